```python
import jax, jax.numpy as jnp
from jax import lax
import numpy as np

D_MODEL = 1024
BATCH = 2
SEQ = 8192
DEPTH = 4
DEC_BATCH = 128
DEC_SEQ = 1
PAST_LEN = 2048
PAGE_SIZE = 128

N_HEADS = 8
HEAD_DIM = 64
D_ATTN = N_HEADS * HEAD_DIM
IDX_HEADS = 8
IDX_DIM = 64
TOPK_MAX = 256
Q_BLOCK = 128
C_CONV = D_MODEL // 4
CONV_WIDTH = 31
C_POOL = D_MODEL // 4
POOL_WINDOWS = (2, 4, 8, 16)
N_POOL_GROUPS = len(POOL_WINDOWS)
POOL_GROUP = C_POOL // N_POOL_GROUPS
POOL_HIST = max(POOL_WINDOWS) - 1
D_MIX = D_ATTN + C_CONV + C_POOL
D_FF = 2816
EPS = 1e-6
N_MOD = 9
OFF_Q = 0
OFF_K = OFF_Q + D_ATTN
OFF_V = OFF_K + D_ATTN
OFF_QI = OFF_V + D_ATTN
OFF_KI = OFF_QI + IDX_HEADS * IDX_DIM
OFF_WI = OFF_KI + IDX_DIM
OFF_CONV = OFF_WI + IDX_HEADS
OFF_POOL = OFF_CONV + 2 * C_CONV
D_IN = OFF_POOL + C_POOL

kernel_name = "hymba_dsa_conformer_pool_step"


def rmsnorm(x, g):
    xf = x.astype(jnp.float32)
    y = xf * lax.rsqrt(jnp.mean(xf * xf, axis=-1, keepdims=True) + EPS)
    return (y * g.astype(jnp.float32)).astype(x.dtype)


def layernorm(x, g, b):
    xf = x.astype(jnp.float32)
    mu = jnp.mean(xf, axis=-1, keepdims=True)
    var = jnp.mean(jnp.square(xf - mu), axis=-1, keepdims=True)
    return ((xf - mu) * lax.rsqrt(var + EPS) * g.astype(jnp.float32) + b.astype(jnp.float32)).astype(x.dtype)


def modulate(h, shift, scale):
    return h * (1 + scale[:, None, :]) + shift[:, None, :]


def ada_params(c, w, b):
    m = jax.nn.silu(c) @ w + b
    return jnp.split(m, N_MOD, axis=-1)


def half_ffn(x, g, shift, scale, gate, wg, wu, wd):
    h = modulate(rmsnorm(x, g), shift, scale)
    return x + 0.5 * gate[:, None, :] * ((jax.nn.silu(h @ wg) * (h @ wu)) @ wd)


def split_proj(z):
    n, t = z.shape[:2]
    q = z[..., OFF_Q:OFF_K].reshape(n, t, N_HEADS, HEAD_DIM)
    k = z[..., OFF_K:OFF_V].reshape(n, t, N_HEADS, HEAD_DIM)
    v = z[..., OFF_V:OFF_QI].reshape(n, t, N_HEADS, HEAD_DIM)
    qi = z[..., OFF_QI:OFF_KI].reshape(n, t, IDX_HEADS, IDX_DIM)
    ki = z[..., OFF_KI:OFF_WI]
    wi = z[..., OFF_WI:OFF_CONV] * IDX_HEADS ** -0.5
    u = z[..., OFF_CONV:OFF_CONV + C_CONV] * jax.nn.sigmoid(z[..., OFF_CONV + C_CONV:OFF_POOL])
    p = z[..., OFF_POOL:D_IN]
    return q, k, v, qi, ki, wi, u, p


def indexer_scores(qi, wi, ki):
    dots = jnp.einsum('nqhd,nsd->nqhs', qi, ki).astype(jnp.float32) * IDX_DIM ** -0.5
    return jnp.einsum('nqhs,nqh->nqs', jax.nn.relu(dots), wi.astype(jnp.float32))


def select_attend(q, k_sel, v_sel, valid):
    s = jnp.einsum('nqhd,nqkhd->nqhk', q, k_sel).astype(jnp.float32) * HEAD_DIM ** -0.5
    s = jnp.where(valid[:, :, None, :], s, -jnp.inf)
    p = jax.nn.softmax(s, axis=-1).astype(v_sel.dtype)
    return jnp.einsum('nqhk,nqkhd->nqhd', p, v_sel)


_gather_rows = jax.vmap(lambda a, i: a[i])


def dsa_prompt(q, k, v, qi, wi, ki):
    b, t = q.shape[:2]
    topk = min(TOPK_MAX, t // 4)
    nb = t // Q_BLOCK
    key_pos = jnp.arange(t)

    def blocks(a):
        return a.reshape((b, nb, Q_BLOCK) + a.shape[2:]).swapaxes(0, 1)

    def one_block(args):
        qb, qib, wib, blk = args
        q_pos = blk * Q_BLOCK + jnp.arange(Q_BLOCK)
        causal = key_pos[None, :] <= q_pos[:, None]
        sc = jnp.where(causal[None], indexer_scores(qib, wib, ki), -jnp.inf)
        _, idx = lax.top_k(sc, topk)
        valid = idx <= q_pos[None, :, None]
        return select_attend(qb, _gather_rows(k, idx), _gather_rows(v, idx), valid)

    out = lax.map(one_block, (blocks(q), blocks(qi), blocks(wi), jnp.arange(nb)))
    return out.swapaxes(0, 1).reshape(q.shape)


def dsa_sample(q, k_new, v_new, qi, wi, ki_new, cache_k, cache_v, cache_ki, page_table, l):
    n, tn = q.shape[:2]
    past = page_table.shape[1] * PAGE_SIZE
    total = past + tn
    topk = min(TOPK_MAX, total // 4)
    ki_past = cache_ki[l, page_table].reshape(n, past, IDX_DIM)
    ki_all = jnp.concatenate([ki_past, ki_new], axis=1)
    q_pos = past + jnp.arange(tn)
    causal = jnp.arange(total)[None, :] <= q_pos[:, None]
    sc = jnp.where(causal[None], indexer_scores(qi, wi, ki_all), -jnp.inf)
    _, idx = lax.top_k(sc, topk)
    valid = idx <= q_pos[None, :, None]
    in_past = idx < past
    pidx = jnp.minimum(idx, past - 1)
    phys = jnp.take_along_axis(page_table, (pidx // PAGE_SIZE).reshape(n, -1), axis=1).reshape(idx.shape)
    off = pidx % PAGE_SIZE
    nidx = jnp.clip(idx - past, 0, tn - 1)
    k_sel = jnp.where(in_past[..., None, None], cache_k[l, phys, off], _gather_rows(k_new, nidx))
    v_sel = jnp.where(in_past[..., None, None], cache_v[l, phys, off], _gather_rows(v_new, nidx))
    return select_attend(q, k_sel, v_sel, valid)


def conv_module(u_ext, dw_w, dw_b, ln_g, ln_b, pw_w):
    y = lax.conv_general_dilated(u_ext, dw_w[:, None, :], (1,), 'VALID',
                                 dimension_numbers=('NWC', 'WIO', 'NWC'),
                                 feature_group_count=C_CONV) + dw_b
    y = jax.nn.silu(layernorm(y, ln_g, ln_b))
    return y @ pw_w


def pool_module(p_ext, pos0, w_grp, scale):
    t = p_ext.shape[1] - POOL_HIST
    cs = jnp.pad(jnp.cumsum(p_ext.astype(jnp.float32), axis=1), ((0, 0), (1, 0), (0, 0)))
    pos = pos0 + jnp.arange(t)
    cur = p_ext[:, POOL_HIST:].astype(jnp.float32)
    outs = []
    for g, w in enumerate(POOL_WINDOWS):
        sl = slice(g * POOL_GROUP, (g + 1) * POOL_GROUP)
        win_sum = cs[:, POOL_HIST + 1:, sl] - cs[:, POOL_HIST + 1 - w:POOL_HIST + 1 - w + t, sl]
        cnt = jnp.minimum(pos + 1, w).astype(jnp.float32)[None, :, None]
        outs.append(win_sum / cnt - cur[..., sl])
    d = jnp.stack(outs, axis=2).astype(p_ext.dtype)
    y = jnp.einsum('ntgc,gce->ntge', d, w_grp).reshape(d.shape[0], t, C_POOL)
    return y * scale


def merge_heads(a, cv, pl, w_o):
    n, t = cv.shape[:2]
    return jnp.concatenate([a.reshape(n, t, D_ATTN), cv, pl], axis=-1) @ w_o


def setup_inputs(seed: int = 0) -> dict:
    key = jax.random.key(seed)
    ks = jax.random.split(key, 32)
    n_pages = PAST_LEN // PAGE_SIZE
    n_pool = (5 * DEC_BATCH * n_pages + 3) // 4

    def nrm(k, shape, s):
        return jax.random.normal(k, shape, jnp.float32) * s

    page_table = jax.random.permutation(ks[7], n_pool)[:DEC_BATCH * n_pages].reshape(DEC_BATCH, n_pages).astype(jnp.int32)
    return {
        "x_prompt": nrm(ks[0], (BATCH, SEQ, D_MODEL), 1.0),
        "x_sample": nrm(ks[1], (DEC_BATCH, DEC_SEQ, D_MODEL), 1.0),
        "cache_k": nrm(ks[2], (DEPTH, n_pool, PAGE_SIZE, N_HEADS, HEAD_DIM), 1.0),
        "cache_v": nrm(ks[3], (DEPTH, n_pool, PAGE_SIZE, N_HEADS, HEAD_DIM), 1.0),
        "cache_ki": nrm(ks[4], (DEPTH, n_pool, PAGE_SIZE, IDX_DIM), 1.0),
        "state_conv": nrm(ks[5], (DEPTH, DEC_BATCH, CONV_WIDTH - 1, C_CONV), 0.5),
        "state_pool": nrm(ks[6], (DEPTH, DEC_BATCH, POOL_HIST, C_POOL), 1.0),
        "page_table": page_table,
        "c_prompt": nrm(ks[8], (BATCH, D_MODEL), 1.0),
        "c_sample": nrm(ks[9], (DEC_BATCH, D_MODEL), 1.0),
        "ada_w": nrm(ks[10], (DEPTH, D_MODEL, N_MOD * D_MODEL), 0.5 * D_MODEL ** -0.5),
        "ada_b": nrm(ks[11], (DEPTH, N_MOD * D_MODEL), 0.02),
        "norm_g": 1.0 + nrm(ks[12], (DEPTH, 3, D_MODEL), 0.05),
        "ffn_wg": nrm(ks[13], (DEPTH, 2, D_MODEL, D_FF), D_MODEL ** -0.5),
        "ffn_wu": nrm(ks[14], (DEPTH, 2, D_MODEL, D_FF), D_MODEL ** -0.5),
        "ffn_wd": nrm(ks[15], (DEPTH, 2, D_FF, D_MODEL), D_FF ** -0.5),
        "w_in": nrm(ks[16], (DEPTH, D_MODEL, D_IN), D_MODEL ** -0.5),
        "w_out": nrm(ks[17], (DEPTH, D_MIX, D_MODEL), D_MIX ** -0.5),
        "conv_dw_w": nrm(ks[18], (DEPTH, CONV_WIDTH, C_CONV), CONV_WIDTH ** -0.5),
        "conv_dw_b": nrm(ks[19], (DEPTH, C_CONV), 0.02),
        "conv_ln_g": 1.0 + nrm(ks[20], (DEPTH, C_CONV), 0.05),
        "conv_ln_b": nrm(ks[21], (DEPTH, C_CONV), 0.02),
        "conv_pw_w": nrm(ks[22], (DEPTH, C_CONV, C_CONV), C_CONV ** -0.5),
        "pool_w": nrm(ks[23], (DEPTH, N_POOL_GROUPS, POOL_GROUP, POOL_GROUP), POOL_GROUP ** -0.5),
        "pool_scale": 1.0 + nrm(ks[24], (DEPTH, C_POOL), 0.05),
        "final_g": 1.0 + nrm(ks[25], (D_MODEL,), 0.05),
    }


def reference(x_prompt, x_sample, cache_k, cache_v, cache_ki, state_conv, state_pool, page_table,
              c_prompt, c_sample, ada_w, ada_b, norm_g, ffn_wg, ffn_wu, ffn_wd, w_in, w_out,
              conv_dw_w, conv_dw_b, conv_ln_g, conv_ln_b, conv_pw_w, pool_w, pool_scale, final_g):
    past = page_table.shape[1] * PAGE_SIZE
    xp, xs = x_prompt, x_sample
    pk, pv, pki, pconv, ppool = [], [], [], [], []
    sk, sv, ski, sconv, spool = [], [], [], [], []
    for l in range(DEPTH):
        mp = ada_params(c_prompt, ada_w[l], ada_b[l])
        ms = ada_params(c_sample, ada_w[l], ada_b[l])
        xp = half_ffn(xp, norm_g[l, 0], mp[0], mp[1], mp[2], ffn_wg[l, 0], ffn_wu[l, 0], ffn_wd[l, 0])
        xs = half_ffn(xs, norm_g[l, 0], ms[0], ms[1], ms[2], ffn_wg[l, 0], ffn_wu[l, 0], ffn_wd[l, 0])

        q, k, v, qi, ki, wi, u, p = split_proj(modulate(rmsnorm(xp, norm_g[l, 1]), mp[3], mp[4]) @ w_in[l])
        a = dsa_prompt(q, k, v, qi, wi, ki)
        u_ext = jnp.concatenate([jnp.zeros((u.shape[0], CONV_WIDTH - 1, C_CONV), u.dtype), u], axis=1)
        p_ext = jnp.concatenate([jnp.zeros((p.shape[0], POOL_HIST, C_POOL), p.dtype), p], axis=1)
        cv = conv_module(u_ext, conv_dw_w[l], conv_dw_b[l], conv_ln_g[l], conv_ln_b[l], conv_pw_w[l])
        pl = pool_module(p_ext, 0, pool_w[l], pool_scale[l])
        xp = xp + mp[5][:, None, :] * merge_heads(a, cv, pl, w_out[l])
        pk.append(k); pv.append(v); pki.append(ki)
        pconv.append(u_ext[:, -(CONV_WIDTH - 1):]); ppool.append(p_ext[:, -POOL_HIST:])

        q, k, v, qi, ki, wi, u, p = split_proj(modulate(rmsnorm(xs, norm_g[l, 1]), ms[3], ms[4]) @ w_in[l])
        a = dsa_sample(q, k, v, qi, wi, ki, cache_k, cache_v, cache_ki, page_table, l)
        u_ext = jnp.concatenate([state_conv[l], u], axis=1)
        p_ext = jnp.concatenate([state_pool[l], p], axis=1)
        cv = conv_module(u_ext, conv_dw_w[l], conv_dw_b[l], conv_ln_g[l], conv_ln_b[l], conv_pw_w[l])
        pl = pool_module(p_ext, past, pool_w[l], pool_scale[l])
        xs = xs + ms[5][:, None, :] * merge_heads(a, cv, pl, w_out[l])
        sk.append(k); sv.append(v); ski.append(ki)
        sconv.append(u_ext[:, -(CONV_WIDTH - 1):]); spool.append(p_ext[:, -POOL_HIST:])

        xp = half_ffn(xp, norm_g[l, 2], mp[6], mp[7], mp[8], ffn_wg[l, 1], ffn_wu[l, 1], ffn_wd[l, 1])
        xs = half_ffn(xs, norm_g[l, 2], ms[6], ms[7], ms[8], ffn_wg[l, 1], ffn_wu[l, 1], ffn_wd[l, 1])

    y_prompt = rmsnorm(xp, final_g)
    y_sample = rmsnorm(xs, final_g)
    return (y_prompt, y_sample,
            jnp.stack(pk), jnp.stack(pv), jnp.stack(pki), jnp.stack(pconv), jnp.stack(ppool),
            jnp.stack(sk), jnp.stack(sv), jnp.stack(ski), jnp.stack(sconv), jnp.stack(spool))
```

```python
import functools
import math

import jax
import jax.numpy as jnp
from jax import lax
from jax.experimental import pallas as pl
from jax.experimental.pallas import tpu as pltpu

F32 = jnp.float32
BF16 = jnp.bfloat16
I32 = jnp.int32

D_MODEL = 1024
N_HEADS = 8
HEAD_DIM = 64
D_ATTN = N_HEADS * HEAD_DIM
IDX_HEADS = 8
IDX_DIM = 64
TOPK_MAX = 256
PAGE_SIZE = 128
C_CONV = 256
CONV_WIDTH = 31
C_POOL = 256
POOL_WINDOWS = (2, 4, 8, 16)
POOL_GROUP = C_POOL // len(POOL_WINDOWS)
POOL_HIST = max(POOL_WINDOWS) - 1
D_FF = 2816
EPS = 1e-6
N_MOD = 9
OFF_Q = 0
OFF_K = OFF_Q + D_ATTN
OFF_V = OFF_K + D_ATTN
OFF_QI = OFF_V + D_ATTN
OFF_KI = OFF_QI + IDX_HEADS * IDX_DIM
OFF_WI = OFF_KI + IDX_DIM
OFF_CONV = OFF_WI + IDX_HEADS
OFF_POOL = OFF_CONV + 2 * C_CONV
D_IN = OFF_POOL + C_POOL

LANES = 128
Q_TILE = 128
KEY_TILE = 512
TOKEN_TILE = 512
FF_TILE = 1408
HALO = 32
MASKED = -1e30
VMEM_LIMIT = 48 * 1024 * 1024

C_Q, C_K, C_V, C_QI, C_KI2, C_CA, C_CG, C_P = 0, 512, 1024, 1536, 2048, 2176, 2432, 2688
D_CAT = 2944


def _sigmoid(x):
    return 1.0 / (1.0 + jnp.exp(-x))


def _dot(a, b):
    return jnp.dot(a, b, preferred_element_type=F32)


def _dot_nt(a, b):
    return lax.dot_general(a, b, (((1,), (1,)), ((), ())), preferred_element_type=F32)


def _params(sem, vmem=VMEM_LIMIT):
    return pltpu.CompilerParams(dimension_semantics=sem, vmem_limit_bytes=vmem)


def _ada_kernel(c_ref, w_ref, b_ref, o_ref):
    c = c_ref[...]
    a = (c * _sigmoid(c)).astype(BF16)
    o_ref[...] = _dot(a, w_ref[...].astype(BF16)) + b_ref[...]


def _ada(c_all, ada_w, ada_b):
    depth, d, n = ada_w.shape
    r = c_all.shape[0]
    tn = 1024
    return pl.pallas_call(
        _ada_kernel,
        out_shape=jax.ShapeDtypeStruct((depth, r, n), F32),
        grid=(depth, n // tn),
        in_specs=[
            pl.BlockSpec((r, d), lambda l, j: (0, 0)),
            pl.BlockSpec((None, d, tn), lambda l, j: (l, 0, j)),
            pl.BlockSpec((None, 1, tn), lambda l, j: (l, 0, j)),
        ],
        out_specs=pl.BlockSpec((None, r, tn), lambda l, j: (l, 0, j)),
        compiler_params=_params(("parallel", "parallel")),
        name="ada",
    )(c_all, ada_w, ada_b.reshape(depth, 1, n))


def _norm_mod(x, g, shift, scale):
    y = x * lax.rsqrt(jnp.mean(x * x, axis=-1, keepdims=True) + EPS) * g
    return y * (1.0 + scale) + shift


def _ffn_kernel(x_ref, g_ref, sh_ref, sc_ref, gt_ref, wg_ref, wu_ref, wd_ref, o_ref, h_ref, acc_ref):
    j = pl.program_id(1)

    @pl.when(j == 0)
    def _():
        h_ref[...] = _norm_mod(x_ref[...], g_ref[...], sh_ref[...], sc_ref[...]).astype(BF16)

    h = h_ref[...]
    g = _dot(h, wg_ref[...])
    u = _dot(h, wu_ref[...])
    part = _dot(((g * _sigmoid(g)) * u).astype(BF16), wd_ref[...])

    @pl.when(j == 0)
    def _():
        acc_ref[...] = part

    @pl.when(j > 0)
    def _():
        acc_ref[...] += part

    @pl.when(j == pl.num_programs(1) - 1)
    def _():
        o_ref[...] = x_ref[...] + (0.5 * gt_ref[...]) * acc_ref[...]


def _mod_spec(mod, tiles_per_group):
    _, r, d = mod.shape
    return pl.BlockSpec((None, r, d), lambda i, *_: (i // tiles_per_group, 0, 0))


def _ffn(x, g, shift, scale, gate, wg, wu, wd, l, s):
    n, d = x.shape
    tm = min(TOKEN_TILE, n)
    tpg = (n // shift.shape[0]) // tm
    nf = D_FF // FF_TILE
    return pl.pallas_call(
        _ffn_kernel,
        out_shape=jax.ShapeDtypeStruct((n, d), F32),
        grid=(n // tm, nf),
        in_specs=[
            pl.BlockSpec((tm, d), lambda i, j: (i, 0)),
            pl.BlockSpec((1, d), lambda i, j: (0, 0)),
            _mod_spec(shift, tpg), _mod_spec(scale, tpg), _mod_spec(gate, tpg),
            pl.BlockSpec((None, None, d, FF_TILE), lambda i, j: (l, s, 0, j)),
            pl.BlockSpec((None, None, d, FF_TILE), lambda i, j: (l, s, 0, j)),
            pl.BlockSpec((None, None, FF_TILE, d), lambda i, j: (l, s, j, 0)),
        ],
        out_specs=pl.BlockSpec((tm, d), lambda i, j: (i, 0)),
        scratch_shapes=[pltpu.VMEM((tm, d), BF16), pltpu.VMEM((tm, d), F32)],
        compiler_params=_params(("parallel", "arbitrary")),
        name="ffn",
    )(x, g, shift, scale, gate, wg, wu, wd)


def _inproj_kernel(x_ref, g_ref, sh_ref, sc_ref, wcat_ref, wvt_ref, wwit_ref,
                   q_ref, kf_ref, kb_ref, vf_ref, vt_ref, qi_ref, kif_ref, ki2_ref, wit_ref, u_ref, p_ref):
    h = _norm_mod(x_ref[...], g_ref[...], sh_ref[...], sc_ref[...]).astype(BF16)
    z = _dot(h, wcat_ref[...])
    q_ref[...] = (z[:, C_Q:C_K] * HEAD_DIM ** -0.5).astype(BF16)
    k = z[:, C_K:C_V]
    kf_ref[...] = k
    kb_ref[...] = k.astype(BF16)
    vf_ref[...] = z[:, C_V:C_QI]
    vt_ref[...] = _dot_nt(wvt_ref[...], h).astype(BF16)
    qi_ref[...] = z[:, C_QI:C_KI2].astype(BF16)
    ki2 = z[:, C_KI2:C_CA]
    kif_ref[...] = ki2[:, :IDX_DIM]
    ki2_ref[...] = ki2.astype(BF16)
    wit_ref[...] = _dot_nt(wwit_ref[...], h) * IDX_HEADS ** -0.5 * IDX_DIM ** -0.5
    u_ref[...] = z[:, C_CA:C_CG] * _sigmoid(z[:, C_CG:C_P])
    p_ref[...] = z[:, C_P:D_CAT]


def _inproj(x, g, shift, scale, wcat, wvt, wwit, l):
    n, d = x.shape
    tm = min(TOKEN_TILE, n)
    tpg = (n // shift.shape[0]) // tm
    nt = n // tm
    row = lambda w: pl.BlockSpec((tm, w), lambda i: (i, 0))
    out_shape = (
        jax.ShapeDtypeStruct((n, D_ATTN), BF16),
        jax.ShapeDtypeStruct((n, D_ATTN), F32),
        jax.ShapeDtypeStruct((n, D_ATTN), BF16),
        jax.ShapeDtypeStruct((n, D_ATTN), F32),
        jax.ShapeDtypeStruct((nt, D_ATTN, tm), BF16),
        jax.ShapeDtypeStruct((n, D_ATTN), BF16),
        jax.ShapeDtypeStruct((n, IDX_DIM), F32),
        jax.ShapeDtypeStruct((n, 2 * IDX_DIM), BF16),
        jax.ShapeDtypeStruct((IDX_HEADS, n), F32),
        jax.ShapeDtypeStruct((n, C_CONV), F32),
        jax.ShapeDtypeStruct((n, C_POOL), F32),
    )
    out_specs = (
        row(D_ATTN), row(D_ATTN), row(D_ATTN), row(D_ATTN),
        pl.BlockSpec((None, D_ATTN, tm), lambda i: (i, 0, 0)),
        row(D_ATTN), row(IDX_DIM), row(2 * IDX_DIM),
        pl.BlockSpec((IDX_HEADS, tm), lambda i: (0, i)),
        row(C_CONV), row(C_POOL),
    )
    return pl.pallas_call(
        _inproj_kernel,
        out_shape=out_shape,
        grid=(nt,),
        in_specs=[
            pl.BlockSpec((tm, d), lambda i: (i, 0)),
            pl.BlockSpec((1, d), lambda i: (0, 0)),
            _mod_spec(shift, tpg), _mod_spec(scale, tpg),
            pl.BlockSpec((None, d, D_CAT), lambda i: (l, 0, 0)),
            pl.BlockSpec((None, D_ATTN, d), lambda i: (l, 0, 0)),
            pl.BlockSpec((None, IDX_HEADS, d), lambda i: (l, 0, 0)),
        ],
        out_specs=out_specs,
        compiler_params=_params(("parallel",)),
        name="inproj",
    )(x, g, shift, scale, wcat, wvt, wwit)


def _key_to_float(key):
    bits = jnp.where(key < 0, key ^ jnp.int32(-2 ** 31), ~key)
    return lax.bitcast_convert_type(bits, F32)


def _kth_largest(count_ge, shape, k):
    def step(i, key):
        cand = key | jnp.left_shift(jnp.int32(1), (31 - i).astype(I32))
        return jnp.where(count_ge(_key_to_float(cand)) >= k, cand, key)

    return _key_to_float(lax.fori_loop(0, 32, step, jnp.zeros(shape, I32)))


def _tie_cut(count_eq_below, need, shape, nbits):
    def step(i, x):
        cand = x | jnp.left_shift(jnp.int32(1), (nbits - 1 - i).astype(I32))
        return jnp.where(count_eq_below(cand) < need, cand, x)

    return lax.fori_loop(0, nbits, step, jnp.zeros(shape, I32))


def _select_bias(s, idx, thr, cut):
    return jnp.where(s > thr, 0.0, jnp.where(s == thr, jnp.where(idx <= cut, 0.0, MASKED), MASKED))


def _head_pairs(x):
    lo = lax.broadcasted_iota(I32, (Q_TILE, LANES), 1) < HEAD_DIM
    out = []
    for p in range(N_HEADS // 2):
        t = x[:, p * LANES:(p + 1) * LANES]
        z = jnp.zeros_like(t)
        out.append(jnp.concatenate([jnp.where(lo, t, z), jnp.where(lo, z, t)], axis=0))
    return out


def _dsa_prompt_kernel(q_ref, qi_ref, wit_ref, k_ref, ki2_ref, vt_ref, o_ref,
                       s_ref, acc_ref, thr_ref, cut_ref, *, topk, idx_bits):
    j = pl.program_id(1)
    tk = KEY_TILE
    nch = ((j + 1) * Q_TILE + tk - 1) // tk
    npair = N_HEADS // 2

    qi_p = _head_pairs(qi_ref[...])
    q_p = _head_pairs(q_ref[...])
    wt = wit_ref[...]
    wrow = [jnp.concatenate([wt[2 * p:2 * p + 1], wt[2 * p + 1:2 * p + 2]], axis=1) for p in range(npair)]
    qpos = j * Q_TILE + lax.broadcasted_iota(I32, (tk, LANES), 1)
    krow = lax.broadcasted_iota(I32, (tk, LANES), 0)

    def tile(c):
        return pl.ds(pl.multiple_of(c * tk, tk), tk)

    def scores(c, carry):
        kic = ki2_ref[tile(c), :]
        acc = jnp.zeros((tk, LANES), F32)
        for p in range(npair):
            r = jnp.maximum(_dot_nt(kic, qi_p[p]), 0.0) * wrow[p]
            acc = acc + (r[:, :LANES] + r[:, LANES:])
        s_ref[tile(c), :] = jnp.where(krow + c * tk <= qpos, acc, -jnp.inf)
        return carry

    lax.fori_loop(0, nch, scores, 0)

    def count(pred):
        def body(c, acc):
            return acc + jnp.sum(pred(s_ref[tile(c), :], krow + c * tk), axis=0, keepdims=True)
        return lax.fori_loop(0, nch, body, jnp.zeros((1, LANES), F32))

    thr_ref[...] = jnp.full(thr_ref.shape, -jnp.inf, F32)
    cut_ref[...] = jnp.full(cut_ref.shape, -1, I32)

    @pl.when((j + 1) * Q_TILE > topk)
    def _():
        thr = _kth_largest(lambda t: count(lambda s, i: jnp.where(s >= t, 1.0, 0.0)), (1, LANES), topk)
        n_gt = count(lambda s, i: jnp.where(s > thr, 1.0, 0.0))
        n_eq = count(lambda s, i: jnp.where(s == thr, 1.0, 0.0))
        need = topk - n_gt
        thr_ref[...] = jnp.broadcast_to(thr, thr_ref.shape)
        cut_ref[...] = jnp.full(cut_ref.shape, 2 ** idx_bits, I32)

        @pl.when(jnp.max(n_eq - need) > 0.0)
        def _():
            cut = _tie_cut(
                lambda x: count(lambda s, i: jnp.where(s == thr, jnp.where(i < x, 1.0, 0.0), 0.0)),
                need, (1, LANES), idx_bits)
            cut_ref[...] = jnp.broadcast_to(cut, cut_ref.shape)

    thr = thr_ref[0:1, :]
    cut = cut_ref[0:1, :]

    acc_ref[...] = jnp.zeros(acc_ref.shape, F32)

    def attend(c, carry):
        ms, ls = carry
        bias = _select_bias(s_ref[tile(c), :], krow + c * tk, thr, cut)
        bias = jnp.concatenate([bias, bias], axis=1)
        kc = k_ref[tile(c), :]
        new_m, new_l = [], []
        for p in range(npair):
            s = _dot_nt(kc[:, p * LANES:(p + 1) * LANES], q_p[p]) + bias
            m = jnp.maximum(ms[p], jnp.max(s, axis=0, keepdims=True))
            alpha = jnp.exp(ms[p] - m)
            e = jnp.exp(s - m)
            new_l.append(alpha * ls[p] + jnp.sum(e, axis=0, keepdims=True))
            new_m.append(m)
            vt = vt_ref[c, p * LANES:(p + 1) * LANES, :]
            acc_ref[p] = alpha * acc_ref[p] + _dot(vt, e.astype(BF16))
        return tuple(new_m), tuple(new_l)

    init = (tuple(jnp.full((1, 2 * LANES), MASKED, F32) for _ in range(npair)),
            tuple(jnp.zeros((1, 2 * LANES), F32) for _ in range(npair)))
    _, ls = lax.fori_loop(0, nch, attend, init)

    even = lax.broadcasted_iota(I32, (LANES, LANES), 0) < HEAD_DIM
    for p in range(npair):
        o = acc_ref[p] / ls[p]
        ot = jnp.where(even, o[:, :LANES], o[:, LANES:])
        o_ref[:, p * LANES:(p + 1) * LANES] = ot.T.astype(o_ref.dtype)


def _dsa_prompt(q, qi, wit, kb, ki2, vt, batch):
    n = q.shape[0]
    t = n // batch
    nq = t // Q_TILE
    ntk = t // KEY_TILE
    topk = min(TOPK_MAX, t // 4)
    idx_bits = max(1, math.ceil(math.log2(t)))
    once = dict(pipeline_mode=pl.Buffered(1))
    return pl.pallas_call(
        functools.partial(_dsa_prompt_kernel, topk=topk, idx_bits=idx_bits),
        out_shape=jax.ShapeDtypeStruct((n, D_ATTN), BF16),
        grid=(batch, nq),
        in_specs=[
            pl.BlockSpec((Q_TILE, D_ATTN), lambda b, j: (b * nq + j, 0)),
            pl.BlockSpec((Q_TILE, D_ATTN), lambda b, j: (b * nq + j, 0)),
            pl.BlockSpec((IDX_HEADS, Q_TILE), lambda b, j: (0, b * nq + j)),
            pl.BlockSpec((None, t, D_ATTN), lambda b, j: (b, 0, 0), **once),
            pl.BlockSpec((None, t, 2 * IDX_DIM), lambda b, j: (b, 0, 0), **once),
            pl.BlockSpec((None, ntk, D_ATTN, KEY_TILE), lambda b, j: (b, 0, 0, 0), **once),
        ],
        out_specs=pl.BlockSpec((Q_TILE, D_ATTN), lambda b, j: (b * nq + j, 0)),
        scratch_shapes=[
            pltpu.VMEM((t, LANES), F32),
            pltpu.VMEM((N_HEADS // 2, LANES, 2 * LANES), F32),
            pltpu.VMEM((8, LANES), F32),
            pltpu.VMEM((8, LANES), I32),
        ],
        compiler_params=_params(("parallel", "arbitrary")),
        name="dsa_prompt",
    )(q, qi, wit, kb.reshape(batch, t, D_ATTN), ki2.reshape(batch, t, 2 * IDX_DIM),
      vt.reshape(batch, ntk, D_ATTN, KEY_TILE))


def _conv_tail(y, dwb_ref, lng_ref, lnb_ref, pww_ref):
    y = y + dwb_ref[...]
    mu = jnp.mean(y, axis=-1, keepdims=True)
    var = jnp.mean(jnp.square(y - mu), axis=-1, keepdims=True)
    y = (y - mu) * lax.rsqrt(var + EPS) * lng_ref[...] + lnb_ref[...]
    return _dot((y * _sigmoid(y)).astype(BF16), pww_ref[...])


def _merge(a, cv, yp, x, gate, wo_ref):
    m = _dot(a, wo_ref[0:D_ATTN, :])
    m = m + _dot(cv.astype(BF16), wo_ref[D_ATTN:D_ATTN + C_CONV, :])
    m = m + _dot(yp.astype(BF16), wo_ref[D_ATTN + C_CONV:, :])
    return x + gate * m


def _pool_window_lanes(shape):
    grp = lax.broadcasted_iota(I32, shape, len(shape) - 1) // POOL_GROUP
    w = jnp.full(shape, POOL_WINDOWS[-1], I32)
    for gi in range(len(POOL_WINDOWS) - 2, -1, -1):
        w = jnp.where(grp == gi, POOL_WINDOWS[gi], w)
    return grp, w


def _mix_kernel(a_ref, u_ref, uh_ref, p_ref, ph_ref, x_ref, gate_ref, dww_ref, dwb_ref, lng_ref, lnb_ref,
                pww_ref, pbd_ref, psc_ref, wo_ref, o_ref, ub_ref, pb_ref, pc_ref, *, tiles_per_seq):
    tm = u_ref.shape[0]
    i = pl.program_id(0)
    ti = i % tiles_per_seq
    first = ti == 0
    pad = 16

    ub_ref[0:HALO, :] = jnp.where(first, 0.0, uh_ref[...])
    ub_ref[HALO:HALO + tm, :] = u_ref[...]
    y = jnp.zeros((tm, C_CONV), F32)
    for tap in range(CONV_WIDTH):
        y = y + dww_ref[tap:tap + 1, :] * ub_ref[pl.ds(HALO - (CONV_WIDTH - 1) + tap, tm), :]
    cv = _conv_tail(y, dwb_ref, lng_ref, lnb_ref, pww_ref)

    n = HALO + tm
    pb_ref[0:pad, :] = jnp.zeros((pad, C_POOL), F32)
    pc_ref[0:pad, :] = jnp.zeros((pad, C_POOL), F32)
    pb_ref[pad:pad + HALO, :] = jnp.where(first, 0.0, ph_ref[...])
    pb_ref[pad + HALO:pad + n, :] = p_ref[...]
    cur = p_ref[...]
    grp, wl = _pool_window_lanes((tm, C_POOL))
    src, dst = pb_ref, pc_ref
    win = jnp.zeros((tm, C_POOL), F32)
    for gi, w in enumerate(POOL_WINDOWS):
        s = w // 2
        dst[pad:pad + n, :] = src[pl.ds(pad, n), :] + src[pl.ds(pad - s, n), :]
        win = jnp.where(grp == gi, dst[pl.ds(pad + HALO, tm), :], win)
        src, dst = dst, src
    pos = ti * tm + lax.broadcasted_iota(I32, (tm, C_POOL), 0)
    cnt = jnp.minimum(pos + 1, wl).astype(F32)
    d = win / cnt - cur
    yp = _dot(d.astype(BF16), pbd_ref[...]) * psc_ref[...]

    o_ref[...] = _merge(a_ref[...], cv, yp, x_ref[...], gate_ref[...], wo_ref)


def _mix(a, u, p, x, gate, dww, dwb, lng, lnb, pww, pbd, psc, wo, l, batch):
    n, d = x.shape
    tm = TOKEN_TILE
    tps = (n // batch) // tm
    hb = tm // HALO
    halo = lambda w: pl.BlockSpec((HALO, w), lambda i: (jnp.maximum(i * hb - 1, 0), 0))
    lay = lambda *s: pl.BlockSpec((None,) + s, lambda i: (l,) + (0,) * len(s))
    return pl.pallas_call(
        functools.partial(_mix_kernel, tiles_per_seq=tps),
        out_shape=jax.ShapeDtypeStruct((n, d), F32),
        grid=(n // tm,),
        in_specs=[
            pl.BlockSpec((tm, D_ATTN), lambda i: (i, 0)),
            pl.BlockSpec((tm, C_CONV), lambda i: (i, 0)), halo(C_CONV),
            pl.BlockSpec((tm, C_POOL), lambda i: (i, 0)), halo(C_POOL),
            pl.BlockSpec((tm, d), lambda i: (i, 0)),
            _mod_spec(gate, tps),
            lay(CONV_WIDTH, C_CONV), lay(1, C_CONV), lay(1, C_CONV), lay(1, C_CONV),
            lay(C_CONV, C_CONV), lay(C_POOL, C_POOL), lay(1, C_POOL), lay(d, d),
        ],
        out_specs=pl.BlockSpec((tm, d), lambda i: (i, 0)),
        scratch_shapes=[
            pltpu.VMEM((HALO + tm, C_CONV), F32),
            pltpu.VMEM((16 + HALO + tm, C_POOL), F32),
            pltpu.VMEM((16 + HALO + tm, C_POOL), F32),
        ],
        compiler_params=_params(("parallel",)),
        name="mix",
    )(a, u, u, p, p, x, gate, dww, dwb, lng, lnb, pww, pbd, psc, wo)


def _sample_scores_kernel(pt_ref, qi_ref, wi_ref, kin_ref, *rest):
    pages, o_ref = rest[:-1], rest[-1]
    qi = qi_ref[...]
    w = wi_ref[...]
    tiles = []
    for pg in pages:
        d = _dot_nt(qi, pg[...].astype(BF16))
        tiles.append(jnp.sum(jnp.maximum(d, 0.0) * w, axis=0, keepdims=True))
    kin = kin_ref[...].astype(BF16).astype(F32)
    dn = jnp.sum(qi.astype(F32) * kin, axis=1, keepdims=True)
    sn = jnp.sum(jnp.maximum(dn, 0.0) * w, axis=0, keepdims=True)
    lane = lax.broadcasted_iota(I32, (1, LANES), 1)
    tiles.append(jnp.where(lane == 0, sn, -jnp.inf))
    o_ref[...] = jnp.concatenate(tiles, axis=1)


def _page_specs(n_pages, width, l):
    return [pl.BlockSpec((None, None, PAGE_SIZE, width), functools.partial(lambda n, pt, jj: (l, pt[n, jj], 0, 0), jj=jj))
            for jj in range(n_pages)]


def _sample_scores(page_table, qi, wit, kif, cache_ki, l):
    n, n_pages = page_table.shape
    width = n_pages * PAGE_SIZE + LANES
    per_seq = lambda *s: pl.BlockSpec((None,) + s, lambda i, pt: (i,) + (0,) * len(s))
    return pl.pallas_call(
        _sample_scores_kernel,
        out_shape=jax.ShapeDtypeStruct((n, 1, width), F32),
        grid_spec=pltpu.PrefetchScalarGridSpec(
            num_scalar_prefetch=1,
            grid=(n,),
            in_specs=[per_seq(IDX_HEADS, IDX_DIM), per_seq(IDX_HEADS, 1), per_seq(1, IDX_DIM)]
            + _page_specs(n_pages, IDX_DIM, l),
            out_specs=per_seq(1, width),
        ),
        compiler_params=_params(("arbitrary",)),
        name="sample_scores",
    )(page_table, qi.reshape(n, IDX_HEADS, IDX_DIM), wit.T.reshape(n, IDX_HEADS, 1),
      kif.reshape(n, 1, IDX_DIM), *([cache_ki] * n_pages))


def _sample_select_kernel(s_ref, o_ref, *, topk, idx_bits):
    s = s_ref[...]
    idx = lax.broadcasted_iota(I32, s.shape, 1)
    shape = (s.shape[0], 1)

    def count(x):
        return jnp.sum(x, axis=1, keepdims=True)

    thr = _kth_largest(lambda t: count(jnp.where(s >= t, 1.0, 0.0)), shape, topk)
    eq = s == thr
    need = topk - count(jnp.where(s > thr, 1.0, 0.0))
    cut = _tie_cut(lambda x: count(jnp.where(eq, jnp.where(idx < x, 1.0, 0.0), 0.0)), need, shape, idx_bits)
    o_ref[...] = _select_bias(s, idx, thr, cut)


def _sample_select(scores, total):
    n, width = scores.shape
    topk = min(TOPK_MAX, total // 4)
    return pl.pallas_call(
        functools.partial(_sample_select_kernel, topk=topk, idx_bits=max(1, math.ceil(math.log2(width)))),
        out_shape=jax.ShapeDtypeStruct((n, width), F32),
        name="sample_select",
    )(scores)


def _sample_attn_kernel(pt_ref, q_ref, kn_ref, vn_ref, b_ref, *rest):
    n_pages = (len(rest) - 1) // 2
    kp, vp, o_ref = rest[:n_pages], rest[n_pages:2 * n_pages], rest[-1]
    past = n_pages * PAGE_SIZE
    own = (lax.broadcasted_iota(I32, (N_HEADS, D_ATTN), 0)
           == lax.broadcasted_iota(I32, (N_HEADS, D_ATTN), 1) // HEAD_DIM)
    qf = jnp.where(own, jnp.broadcast_to(q_ref[...].astype(F32), (N_HEADS, D_ATTN)), 0.0)
    qm = qf.astype(BF16)
    bias = b_ref[...]
    s = jnp.concatenate([_dot_nt(qm, pg[...].astype(BF16)) for pg in kp], axis=1) + bias[:, :past]
    kn = kn_ref[...].astype(BF16).astype(F32)
    sn = jnp.sum(qf * kn, axis=1, keepdims=True) + bias[:, past:past + 1]
    m = jnp.maximum(jnp.max(s, axis=1, keepdims=True), sn)
    e = jnp.exp(s - m)
    en = jnp.exp(sn - m)
    den = jnp.sum(e, axis=1, keepdims=True) + en
    o = en * vn_ref[...].astype(BF16).astype(F32)
    eb = e.astype(BF16)
    for jj, pg in enumerate(vp):
        o = o + _dot(eb[:, jj * PAGE_SIZE:(jj + 1) * PAGE_SIZE], pg[...].astype(BF16))
    o = o / den
    o_ref[...] = jnp.sum(jnp.where(own, o, 0.0), axis=0, keepdims=True).astype(o_ref.dtype)


def _sample_attn(page_table, q, kf, vf, bias, cache_k, cache_v, l):
    n, n_pages = page_table.shape
    width = bias.shape[-1]
    per_seq = lambda *s: pl.BlockSpec((None,) + s, lambda i, pt: (i,) + (0,) * len(s))
    return pl.pallas_call(
        _sample_attn_kernel,
        out_shape=jax.ShapeDtypeStruct((n, 1, D_ATTN), BF16),
        grid_spec=pltpu.PrefetchScalarGridSpec(
            num_scalar_prefetch=1,
            grid=(n,),
            in_specs=[per_seq(1, D_ATTN), per_seq(1, D_ATTN), per_seq(1, D_ATTN), per_seq(1, width)]
            + _page_specs(n_pages, D_ATTN, l) + _page_specs(n_pages, D_ATTN, l),
            out_specs=per_seq(1, D_ATTN),
        ),
        compiler_params=_params(("arbitrary",)),
        name="sample_attn",
    )(page_table, q.reshape(n, 1, D_ATTN), kf.reshape(n, 1, D_ATTN), vf.reshape(n, 1, D_ATTN),
      bias.reshape(n, 1, width), *([cache_k] * n_pages), *([cache_v] * n_pages))


def _sample_mix_kernel(a_ref, u_ref, sc_ref, p_ref, sp_ref, x_ref, gate_ref, dww_ref, dwb_ref, lng_ref, lnb_ref,
                       pww_ref, pbd_ref, psc_ref, wo_ref, o_ref, *, pos):
    u = u_ref[...]
    hist = CONV_WIDTH - 1
    y = dww_ref[hist:hist + 1, :] * u
    for tap in range(hist):
        y = y + dww_ref[tap:tap + 1, :] * sc_ref[:, tap * C_CONV:(tap + 1) * C_CONV]
    cv = _conv_tail(y, dwb_ref, lng_ref, lnb_ref, pww_ref)

    cur = p_ref[...]
    grp, wl = _pool_window_lanes(cur.shape)
    run = cur
    win = jnp.zeros_like(cur)
    back = 0
    for gi, w in enumerate(POOL_WINDOWS):
        while back < w - 1:
            back += 1
            run = run + sp_ref[:, (POOL_HIST - back) * C_POOL:(POOL_HIST - back + 1) * C_POOL]
        win = jnp.where(grp == gi, run, win)
    cnt = jnp.minimum(pos + 1, wl).astype(F32)
    d = win / cnt - cur
    yp = _dot(d.astype(BF16), pbd_ref[...]) * psc_ref[...]

    o_ref[...] = _merge(a_ref[...], cv, yp, x_ref[...], gate_ref[...], wo_ref)


def _sample_mix(a, u, state_conv, p, state_pool, x, gate, dww, dwb, lng, lnb, pww, pbd, psc, wo, l, pos):
    n, d = x.shape
    full = lambda arr: pl.BlockSpec(arr.shape, lambda i: (0,) * arr.ndim)
    lay = lambda *s: pl.BlockSpec((None,) + s, lambda i: (l,) + (0,) * len(s))
    sc = state_conv.reshape(state_conv.shape[0], n, (CONV_WIDTH - 1) * C_CONV)
    sp = state_pool.reshape(state_pool.shape[0], n, POOL_HIST * C_POOL)
    return pl.pallas_call(
        functools.partial(_sample_mix_kernel, pos=pos),
        out_shape=jax.ShapeDtypeStruct((n, d), F32),
        grid=(1,),
        in_specs=[
            full(a), full(u), lay(n, (CONV_WIDTH - 1) * C_CONV), full(p), lay(n, POOL_HIST * C_POOL), full(x),
            pl.BlockSpec((None, n, d), lambda i: (0, 0, 0)),
            lay(CONV_WIDTH, C_CONV), lay(1, C_CONV), lay(1, C_CONV), lay(1, C_CONV),
            lay(C_CONV, C_CONV), lay(C_POOL, C_POOL), lay(1, C_POOL), lay(d, d),
        ],
        out_specs=full(x),
        compiler_params=_params(("arbitrary",)),
        name="sample_mix",
    )(a, u, sc, p, sp, x, gate, dww, dwb, lng, lnb, pww, pbd, psc, wo)


def _final_norm_kernel(x_ref, g_ref, o_ref):
    x = x_ref[...]
    o_ref[...] = x * lax.rsqrt(jnp.mean(x * x, axis=-1, keepdims=True) + EPS) * g_ref[...]


def _final_norm(x, g):
    n, d = x.shape
    tm = min(TOKEN_TILE, n)
    return pl.pallas_call(
        _final_norm_kernel,
        out_shape=jax.ShapeDtypeStruct((n, d), F32),
        grid=(n // tm,),
        in_specs=[pl.BlockSpec((tm, d), lambda i: (i, 0)), pl.BlockSpec((1, d), lambda i: (0, 0))],
        out_specs=pl.BlockSpec((tm, d), lambda i: (i, 0)),
        compiler_params=_params(("parallel",)),
        name="final_norm",
    )(x, g)


def _prep_weights(w_in, w_out, conv_pw_w, pool_w, ffn_wg, ffn_wu, ffn_wd):
    ki = w_in[:, :, OFF_KI:OFF_WI]
    wcat = jnp.concatenate([
        w_in[:, :, OFF_Q:OFF_KI], ki, ki,
        w_in[:, :, OFF_CONV:OFF_CONV + C_CONV], w_in[:, :, OFF_CONV + C_CONV:OFF_POOL], w_in[:, :, OFF_POOL:D_IN],
    ], axis=-1).astype(BF16)
    wvt = jnp.swapaxes(w_in[:, :, OFF_V:OFF_QI], 1, 2).astype(BF16)
    wwit = jnp.swapaxes(w_in[:, :, OFF_WI:OFF_CONV], 1, 2).astype(BF16)
    depth = pool_w.shape[0]
    pbd = jnp.zeros((depth, C_POOL, C_POOL), F32)
    for gi in range(len(POOL_WINDOWS)):
        sl = slice(gi * POOL_GROUP, (gi + 1) * POOL_GROUP)
        pbd = pbd.at[:, sl, sl].set(pool_w[:, gi])
    return (wcat, wvt, wwit, w_out.astype(BF16), conv_pw_w.astype(BF16), pbd.astype(BF16),
            ffn_wg.astype(BF16), ffn_wu.astype(BF16), ffn_wd.astype(BF16))


def kernel(x_prompt, x_sample, cache_k, cache_v, cache_ki, state_conv, state_pool, page_table, c_prompt, c_sample, ada_w, ada_b, norm_g, ffn_wg, ffn_wu, ffn_wd, w_in, w_out, conv_dw_w, conv_dw_b, conv_ln_g, conv_ln_b, conv_pw_w, pool_w, pool_scale, final_g):
    batch, seq, d = x_prompt.shape
    nseq = x_sample.shape[0]
    depth = w_in.shape[0]
    n_pool = cache_k.shape[1]
    n_pages = page_table.shape[1]
    past = n_pages * PAGE_SIZE
    assert x_sample.shape[1] == 1 and d == D_MODEL
    assert seq % TOKEN_TILE == 0 and TOKEN_TILE == KEY_TILE and nseq % 8 == 0

    wcat, wvt, wwit, wo, pww, pbd, wg, wu, wd = _prep_weights(w_in, w_out, conv_pw_w, pool_w, ffn_wg, ffn_wu, ffn_wd)
    ck = cache_k.reshape(depth, n_pool, PAGE_SIZE, D_ATTN)
    cv_ = cache_v.reshape(depth, n_pool, PAGE_SIZE, D_ATTN)
    dwb = conv_dw_b.reshape(depth, 1, C_CONV)
    lng = conv_ln_g.reshape(depth, 1, C_CONV)
    lnb = conv_ln_b.reshape(depth, 1, C_CONV)
    psc = pool_scale.reshape(depth, 1, C_POOL)

    rows = batch + nseq
    pad = (-rows) % 8
    c_all = jnp.concatenate([c_prompt, c_sample, jnp.zeros((pad, d), F32)], axis=0)
    mods = _ada(c_all, ada_w, ada_b)

    xp = x_prompt.reshape(batch * seq, d)
    xs = x_sample.reshape(nseq, d)
    outs = [[] for _ in range(10)]
    for l in range(depth):
        mp = [mods[l, :batch, i * d:(i + 1) * d].reshape(batch, 1, d) for i in range(N_MOD)]
        ms = [mods[l, batch:rows, i * d:(i + 1) * d].reshape(1, nseq, d) for i in range(N_MOD)]
        g = [norm_g[l, i].reshape(1, d) for i in range(3)]
        mixw = (conv_dw_w, dwb, lng, lnb, pww, pbd, psc, wo, l)

        xp = _ffn(xp, g[0], mp[0], mp[1], mp[2], wg, wu, wd, l, 0)
        xs = _ffn(xs, g[0], ms[0], ms[1], ms[2], wg, wu, wd, l, 0)

        q, kf, kb, vf, vt, qi, kif, ki2, wit, u, p = _inproj(xp, g[1], mp[3], mp[4], wcat, wvt, wwit, l)
        a = _dsa_prompt(q, qi, wit, kb, ki2, vt, batch)
        xp = _mix(a, u, p, xp, mp[5], *mixw, batch)
        u3 = u.reshape(batch, seq, C_CONV)
        p3 = p.reshape(batch, seq, C_POOL)
        for lst, val in zip(outs[:5], (kf.reshape(batch, seq, N_HEADS, HEAD_DIM), vf.reshape(batch, seq, N_HEADS, HEAD_DIM),
                                       kif.reshape(batch, seq, IDX_DIM), u3[:, seq - (CONV_WIDTH - 1):],
                                       p3[:, seq - POOL_HIST:])):
            lst.append(val)

        q, kf, kb, vf, vt, qi, kif, ki2, wit, u, p = _inproj(xs, g[1], ms[3], ms[4], wcat, wvt, wwit, l)
        scores = _sample_scores(page_table, qi, wit, kif, cache_ki, l)
        bias = _sample_select(scores.reshape(nseq, -1), past + 1)
        a = _sample_attn(page_table, q, kf, vf, bias, ck, cv_, l).reshape(nseq, D_ATTN)
        xs = _sample_mix(a, u, state_conv, p, state_pool, xs, ms[5], *mixw, past)
        for lst, val in zip(outs[5:], (kf.reshape(nseq, 1, N_HEADS, HEAD_DIM), vf.reshape(nseq, 1, N_HEADS, HEAD_DIM),
                                       kif.reshape(nseq, 1, IDX_DIM),
                                       jnp.concatenate([state_conv[l][:, 1:], u[:, None, :]], axis=1),
                                       jnp.concatenate([state_pool[l][:, 1:], p[:, None, :]], axis=1))):
            lst.append(val)

        xp = _ffn(xp, g[2], mp[6], mp[7], mp[8], wg, wu, wd, l, 1)
        xs = _ffn(xs, g[2], ms[6], ms[7], ms[8], wg, wu, wd, l, 1)

    fg = final_g.reshape(1, d)
    y_prompt = _final_norm(xp, fg).reshape(batch, seq, d)
    y_sample = _final_norm(xs, fg).reshape(nseq, 1, d)
    return (y_prompt, y_sample) + tuple(jnp.stack(o) for o in outs)
```

```python
import functools
import math

import jax
import jax.numpy as jnp
from jax import lax
from jax.experimental import pallas as pl
from jax.experimental.pallas import tpu as pltpu

F32 = jnp.float32
BF16 = jnp.bfloat16
I32 = jnp.int32

D_MODEL = 1024
N_HEADS = 8
HEAD_DIM = 64
D_ATTN = N_HEADS * HEAD_DIM
IDX_HEADS = 8
IDX_DIM = 64
TOPK_MAX = 256
PAGE_SIZE = 128
C_CONV = 256
CONV_WIDTH = 31
C_POOL = 256
POOL_WINDOWS = (2, 4, 8, 16)
POOL_GROUP = C_POOL // len(POOL_WINDOWS)
POOL_HIST = max(POOL_WINDOWS) - 1
D_FF = 2816
EPS = 1e-6
N_MOD = 9
OFF_Q = 0
OFF_K = OFF_Q + D_ATTN
OFF_V = OFF_K + D_ATTN
OFF_QI = OFF_V + D_ATTN
OFF_KI = OFF_QI + IDX_HEADS * IDX_DIM
OFF_WI = OFF_KI + IDX_DIM
OFF_CONV = OFF_WI + IDX_HEADS
OFF_POOL = OFF_CONV + 2 * C_CONV
D_IN = OFF_POOL + C_POOL

LANES = 128
Q_TILE = 128
KEY_TILE = 512
TOKEN_TILE = 512
FF_TILE = 1408
HALO = 32
MASKED = -1e30
LOWEST = -3.4028234663852886e38
ROW_FOLD = 64
ONES_ROWS = 16
LOG2E = 1.4426950408889634
SHIFT_SLACK = 64.0
VMEM_LIMIT = 48 * 1024 * 1024

C_Q, C_K, C_QI, C_KI2, C_CA, C_CG, C_P = 0, 512, 1024, 1536, 1664, 1920, 2176
D_CAT = 2432
R_K, R_V, R_KI, R_WI = 0, 512, 1024, 1088
R_ALL = 1104


def _sigmoid(x):
    return 1.0 / (1.0 + jnp.exp(-x))


def _dot(a, b):
    return jnp.dot(a, b, preferred_element_type=F32)


def _dot_nt(a, b):
    return lax.dot_general(a, b, (((1,), (1,)), ((), ())), preferred_element_type=F32)


def _params(sem, vmem=VMEM_LIMIT):
    return pltpu.CompilerParams(dimension_semantics=sem, vmem_limit_bytes=vmem)


def _ada_kernel(c_ref, w_ref, b_ref, o_ref):
    c = c_ref[...]
    a = (c * _sigmoid(c)).astype(BF16)
    o_ref[...] = _dot(a, w_ref[...].astype(BF16)) + b_ref[...]


def _ada(c_all, ada_w, ada_b):
    depth, d, n = ada_w.shape
    r = c_all.shape[0]
    tn = 1024
    return pl.pallas_call(
        _ada_kernel,
        out_shape=jax.ShapeDtypeStruct((depth, r, n), F32),
        grid=(depth, n // tn),
        in_specs=[
            pl.BlockSpec((r, d), lambda l, j: (0, 0)),
            pl.BlockSpec((None, d, tn), lambda l, j: (l, 0, j)),
            pl.BlockSpec((None, 1, tn), lambda l, j: (l, 0, j)),
        ],
        out_specs=pl.BlockSpec((None, r, tn), lambda l, j: (l, 0, j)),
        compiler_params=_params(("parallel", "parallel")),
        name="ada",
    )(c_all, ada_w, ada_b.reshape(depth, 1, n))


def _norm_mod(x, g, shift, scale):
    y = x * lax.rsqrt(jnp.mean(x * x, axis=-1, keepdims=True) + EPS) * g
    return y * (1.0 + scale) + shift


def _ffn_kernel(x_ref, g_ref, sh_ref, sc_ref, gt_ref, wg_ref, wu_ref, wd_ref, o_ref, h_ref, acc_ref):
    j = pl.program_id(1)

    @pl.when(j == 0)
    def _():
        h_ref[...] = _norm_mod(x_ref[...], g_ref[...], sh_ref[...], sc_ref[...]).astype(BF16)

    h = h_ref[...]
    g = _dot(h, wg_ref[...])
    u = _dot(h, wu_ref[...])
    part = _dot(((g * _sigmoid(g)) * u).astype(BF16), wd_ref[...])

    @pl.when(j == 0)
    def _():
        acc_ref[...] = part

    @pl.when(j > 0)
    def _():
        acc_ref[...] += part

    @pl.when(j == pl.num_programs(1) - 1)
    def _():
        o_ref[...] = x_ref[...] + (0.5 * gt_ref[...]) * acc_ref[...]


def _mod_spec(mod, tiles_per_group):
    _, r, d = mod.shape
    return pl.BlockSpec((None, r, d), lambda i, *_: (i // tiles_per_group, 0, 0))


def _ffn(x, g, shift, scale, gate, wg, wu, wd, l, s):
    n, d = x.shape
    tm = min(TOKEN_TILE, n)
    tpg = (n // shift.shape[0]) // tm
    nf = D_FF // FF_TILE
    return pl.pallas_call(
        _ffn_kernel,
        out_shape=jax.ShapeDtypeStruct((n, d), F32),
        grid=(n // tm, nf),
        in_specs=[
            pl.BlockSpec((tm, d), lambda i, j: (i, 0)),
            pl.BlockSpec((1, d), lambda i, j: (0, 0)),
            _mod_spec(shift, tpg), _mod_spec(scale, tpg), _mod_spec(gate, tpg),
            pl.BlockSpec((None, None, d, FF_TILE), lambda i, j: (l, s, 0, j)),
            pl.BlockSpec((None, None, d, FF_TILE), lambda i, j: (l, s, 0, j)),
            pl.BlockSpec((None, None, FF_TILE, d), lambda i, j: (l, s, j, 0)),
        ],
        out_specs=pl.BlockSpec((tm, d), lambda i, j: (i, 0)),
        scratch_shapes=[pltpu.VMEM((tm, d), BF16), pltpu.VMEM((tm, d), F32)],
        compiler_params=_params(("parallel", "arbitrary")),
        name="ffn",
    )(x, g, shift, scale, gate, wg, wu, wd)


def _inproj_kernel(x_ref, g_ref, sh_ref, sc_ref, wcat_ref, wt_ref,
                   q_ref, kb_ref, ktf_ref, vtf_ref, vt_ref, qi_ref, kitf_ref, ki2_ref, wit_ref, u_ref, p_ref):
    h = _norm_mod(x_ref[...], g_ref[...], sh_ref[...], sc_ref[...]).astype(BF16)
    z = _dot(h, wcat_ref[...])
    zt = _dot_nt(wt_ref[...], h)
    q_ref[...] = (z[:, C_Q:C_K] * (HEAD_DIM ** -0.5 * LOG2E)).astype(BF16)
    kb_ref[...] = z[:, C_K:C_QI].astype(BF16)
    ktf_ref[...] = zt[R_K:R_V]
    vt = zt[R_V:R_KI]
    vtf_ref[...] = vt
    vt_ref[...] = vt.astype(BF16)
    qi_ref[...] = z[:, C_QI:C_KI2].astype(BF16)
    ki2_ref[...] = z[:, C_KI2:C_CA].astype(BF16)
    kitf_ref[...] = zt[R_KI:R_WI]
    wit_ref[...] = zt[R_WI:R_WI + IDX_HEADS] * IDX_HEADS ** -0.5 * IDX_DIM ** -0.5
    u_ref[...] = z[:, C_CA:C_CG] * _sigmoid(z[:, C_CG:C_P])
    p_ref[...] = z[:, C_P:D_CAT]


def _inproj(x, g, shift, scale, wcat, wt, l):
    n, d = x.shape
    tm = min(TOKEN_TILE, n)
    groups = shift.shape[0]
    tg = n // groups
    tpg = tg // tm
    nt = n // tm
    row = lambda w: pl.BlockSpec((tm, w), lambda i: (i, 0))
    col = lambda w: pl.BlockSpec((None, w, tm), lambda i: (i // tpg, 0, i % tpg))
    out_shape = (
        jax.ShapeDtypeStruct((n, D_ATTN), BF16),
        jax.ShapeDtypeStruct((n, D_ATTN), BF16),
        jax.ShapeDtypeStruct((groups, D_ATTN, tg), F32),
        jax.ShapeDtypeStruct((groups, D_ATTN, tg), F32),
        jax.ShapeDtypeStruct((nt, D_ATTN, tm), BF16),
        jax.ShapeDtypeStruct((n, D_ATTN), BF16),
        jax.ShapeDtypeStruct((groups, IDX_DIM, tg), F32),
        jax.ShapeDtypeStruct((n, 2 * IDX_DIM), BF16),
        jax.ShapeDtypeStruct((IDX_HEADS, n), F32),
        jax.ShapeDtypeStruct((n, C_CONV), F32),
        jax.ShapeDtypeStruct((n, C_POOL), F32),
    )
    out_specs = (
        row(D_ATTN), row(D_ATTN), col(D_ATTN), col(D_ATTN),
        pl.BlockSpec((None, D_ATTN, tm), lambda i: (i, 0, 0)),
        row(D_ATTN), col(IDX_DIM), row(2 * IDX_DIM),
        pl.BlockSpec((IDX_HEADS, tm), lambda i: (0, i)),
        row(C_CONV), row(C_POOL),
    )
    return pl.pallas_call(
        _inproj_kernel,
        out_shape=out_shape,
        grid=(nt,),
        in_specs=[
            pl.BlockSpec((tm, d), lambda i: (i, 0)),
            pl.BlockSpec((1, d), lambda i: (0, 0)),
            _mod_spec(shift, tpg), _mod_spec(scale, tpg),
            pl.BlockSpec((None, d, D_CAT), lambda i: (l, 0, 0)),
            pl.BlockSpec((None, R_ALL, d), lambda i: (l, 0, 0)),
        ],
        out_specs=out_specs,
        compiler_params=_params(("parallel",)),
        name="inproj",
    )(x, g, shift, scale, wcat, wt)


def _key_to_float(key):
    bits = jnp.where(key < 0, key ^ jnp.int32(-2 ** 31), ~key)
    return lax.bitcast_convert_type(bits, F32)


def _kth_largest(count_ge, shape, k):
    def step(i, carry):
        key, n_ge = carry
        cand = key | jnp.left_shift(jnp.int32(1), (31 - i).astype(I32))
        n = count_ge(_key_to_float(cand))
        take = n >= k
        return jnp.where(take, cand, key), jnp.where(take, n, n_ge)

    key, n_ge = lax.fori_loop(0, 32, step, (jnp.zeros(shape, I32), jnp.full(shape, jnp.inf, F32)))
    return _key_to_float(key), n_ge


def _tie_cut(count_eq_below, need, shape, nbits):
    def step(i, x):
        cand = x | jnp.left_shift(jnp.int32(1), (nbits - 1 - i).astype(I32))
        return jnp.where(count_eq_below(cand) < need, cand, x)

    return lax.fori_loop(0, nbits, step, jnp.zeros(shape, I32))


def _select_bias(s, idx, thr, cut):
    return jnp.where(s > thr, 0.0, jnp.where(s == thr, jnp.where(idx <= cut, 0.0, MASKED), MASKED))


def _stage_head_pairs(src_ref, dst_ref):
    lo = lax.broadcasted_iota(I32, (Q_TILE, LANES), 1) < HEAD_DIM
    for p in range(N_HEADS // 2):
        t = src_ref[:, p * LANES:(p + 1) * LANES]
        z = jnp.zeros_like(t)
        dst_ref[p, 0:Q_TILE, 0:LANES] = jnp.where(lo, t, z)
        dst_ref[p, Q_TILE:2 * Q_TILE, 0:LANES] = jnp.where(lo, z, t)


def _fold_rows(x, op):
    rows, lanes = x.shape
    return op(x.reshape(rows // ROW_FOLD, ROW_FOLD, lanes), axis=0)


def _bf16_exact(x):
    bits = lax.bitcast_convert_type(x, I32) & jnp.int32(-65536)
    return lax.bitcast_convert_type(bits, F32)


def _max_to_tile(x):
    x = _fold_rows(x, jnp.max)
    parts = [x[r:r + 8] for r in range(0, ROW_FOLD, 8)]
    while len(parts) > 1:
        parts = [jnp.maximum(parts[i], parts[i + 1]) for i in range(0, len(parts), 2)]
    return parts[0]


def _dsa_prompt_kernel(q_ref, qi_ref, wit_ref, k_ref, ki2_ref, vt_ref, o_ref,
                       s_ref, qp_ref, qip_ref, acc_ref, thr_ref, *, topk, idx_bits):
    j = pl.program_id(1)
    tk = KEY_TILE
    nch = ((j + 1) * Q_TILE + tk - 1) // tk
    npair = N_HEADS // 2

    _stage_head_pairs(qi_ref, qip_ref)
    _stage_head_pairs(q_ref, qp_ref)
    wt = wit_ref[...]
    wrow = [jnp.concatenate([wt[2 * p:2 * p + 1], wt[2 * p + 1:2 * p + 2]], axis=1) for p in range(npair)]
    qpos = j * Q_TILE + lax.broadcasted_iota(I32, (1, LANES), 1)
    krow = lax.broadcasted_iota(I32, (tk, LANES), 0)

    def tile(c):
        return pl.ds(pl.multiple_of(c * tk, tk), tk)

    def scores(c, carry):
        kic = ki2_ref[tile(c), :]
        acc = None
        dots = [_dot_nt(kic, qip_ref[p]) for p in range(npair)]
        for p in range(npair):
            r = jnp.maximum(dots[p], 0.0) * wrow[p]
            r = r[:, :LANES] + r[:, LANES:]
            acc = r if acc is None else acc + r
        s_ref[tile(c), :] = jnp.where(krow <= qpos - c * tk, acc, -jnp.inf)
        return carry

    lax.fori_loop(0, nch, scores, 0)

    def count(pred):
        def body(c, acc):
            return acc + _fold_rows(pred(s_ref[tile(c), :], c), jnp.sum)
        acc = lax.fori_loop(0, nch, body, jnp.zeros((ROW_FOLD, LANES), F32))
        return jnp.sum(acc, axis=0, keepdims=True)

    thr_ref[...] = jnp.full(thr_ref.shape, LOWEST, F32)

    @pl.when((j + 1) * Q_TILE > topk)
    def _():
        thr, n_ge = _kth_largest(lambda t: count(lambda s, c: jnp.where(s >= t, 1.0, 0.0)), (1, LANES), topk)
        thr_ref[...] = jnp.broadcast_to(thr, thr_ref.shape)

        @pl.when(jnp.max(n_ge) > topk)
        def _():
            need = topk - count(lambda s, c: jnp.where(s > thr, 1.0, 0.0))
            cut = _tie_cut(
                lambda x: count(lambda s, c: jnp.where(s == thr, jnp.where(krow + c * tk < x, 1.0, 0.0), 0.0)),
                need, (1, LANES), idx_bits)

            def drop(c, carry):
                s = s_ref[tile(c), :]
                s_ref[tile(c), :] = jnp.where(s == thr, jnp.where(krow + c * tk > cut, -jnp.inf, s), s)
                return carry

            lax.fori_loop(0, nch, drop, 0)

    thr = thr_ref[0:1, :]

    lane = lax.broadcasted_iota(I32, (LANES, LANES), 1)
    ones = jnp.ones((ONES_ROWS, tk), BF16)

    def ones_columns(rows):
        return jnp.where(lax.broadcasted_iota(I32, (rows, LANES), 1) < N_HEADS, 1.0, 0.0).astype(BF16)

    def set_shifts(shifts):
        rows = [sh[:, half * LANES:(half + 1) * LANES] for sh in shifts for half in range(2)]
        cols = jnp.concatenate(rows + [jnp.zeros((LANES - len(rows), LANES), F32)], axis=0).T
        for i in range(len(rows)):
            qp_ref[i // 2, (i % 2) * Q_TILE:(i % 2 + 1) * Q_TILE, LANES:2 * LANES] = (
                jnp.where(lane == i, -cols, 0.0).astype(BF16))

    def shifted_scores(rows, p, bias):
        k_aug = jnp.concatenate([k_ref[rows, p * LANES:(p + 1) * LANES], ones_columns(bias.shape[0])], axis=1)
        return _dot_nt(k_aug, qp_ref[p]) + bias

    def attend_pass():
        acc_ref[...] = jnp.zeros(acc_ref.shape, F32)

        def attend(c, peak):
            bias = jnp.where(s_ref[tile(c), :] >= thr, 0.0, MASKED)
            bias = jnp.concatenate([bias, bias], axis=1)
            new_peak = []
            ss = [shifted_scores(tile(c), p, bias) for p in range(npair)]
            for p in range(npair):
                new_peak.append(jnp.maximum(peak[p], _max_to_tile(ss[p])))
                vt1 = jnp.concatenate([vt_ref[c, p * LANES:(p + 1) * LANES, :], ones], axis=0)
                acc_ref[p] += _dot(vt1, jnp.exp2(ss[p]).astype(BF16))
            return tuple(new_peak)

        peak = lax.fori_loop(0, nch, attend, tuple(jnp.full((8, 2 * LANES), MASKED, F32) for _ in range(npair)))
        return [jnp.max(x, axis=0, keepdims=True) for x in peak]

    bias0 = jnp.where(s_ref[0:LANES, :] > -jnp.inf, 0.0, MASKED)
    bias0 = jnp.concatenate([bias0, bias0], axis=1)
    shifts = []
    for p in range(npair):
        qp_ref[p, :, LANES:2 * LANES] = jnp.zeros((2 * Q_TILE, LANES), BF16)
        sh = jnp.max(shifted_scores(pl.ds(0, LANES), p, bias0), axis=0, keepdims=True)
        shifts.append(_bf16_exact(sh))
    set_shifts(shifts)
    peaks = attend_pass()

    worst = jnp.abs(peaks[0])
    for p in range(1, npair):
        worst = jnp.maximum(worst, jnp.abs(peaks[p]))

    @pl.when(jnp.max(worst) > SHIFT_SLACK)
    def _():
        set_shifts([_bf16_exact(shifts[p] + peaks[p]) for p in range(npair)])
        attend_pass()

    even = lax.broadcasted_iota(I32, (LANES, LANES), 0) < HEAD_DIM
    for p in range(npair):
        o = acc_ref[p, 0:LANES, :] / acc_ref[p, LANES:LANES + 1, :]
        ot = jnp.where(even, o[:, :LANES], o[:, LANES:])
        o_ref[:, p * LANES:(p + 1) * LANES] = ot.T.astype(o_ref.dtype)


def _dsa_prompt(q, qi, wit, kb, ki2, vt, batch):
    n = q.shape[0]
    t = n // batch
    nq = t // Q_TILE
    ntk = t // KEY_TILE
    topk = min(TOPK_MAX, t // 4)
    assert topk % Q_TILE == 0
    idx_bits = max(1, math.ceil(math.log2(t)))
    once = dict(pipeline_mode=pl.Buffered(1))
    return pl.pallas_call(
        functools.partial(_dsa_prompt_kernel, topk=topk, idx_bits=idx_bits),
        out_shape=jax.ShapeDtypeStruct((n, D_ATTN), BF16),
        grid=(batch, nq),
        in_specs=[
            pl.BlockSpec((Q_TILE, D_ATTN), lambda b, j: (b * nq + j, 0)),
            pl.BlockSpec((Q_TILE, D_ATTN), lambda b, j: (b * nq + j, 0)),
            pl.BlockSpec((IDX_HEADS, Q_TILE), lambda b, j: (0, b * nq + j)),
            pl.BlockSpec((None, t, D_ATTN), lambda b, j: (b, 0, 0), **once),
            pl.BlockSpec((None, t, 2 * IDX_DIM), lambda b, j: (b, 0, 0), **once),
            pl.BlockSpec((None, ntk, D_ATTN, KEY_TILE), lambda b, j: (b, 0, 0, 0), **once),
        ],
        out_specs=pl.BlockSpec((Q_TILE, D_ATTN), lambda b, j: (b * nq + j, 0)),
        scratch_shapes=[
            pltpu.VMEM((t, LANES), F32),
            pltpu.VMEM((N_HEADS // 2, 2 * Q_TILE, 2 * LANES), BF16),
            pltpu.VMEM((N_HEADS // 2, 2 * Q_TILE, LANES), BF16),
            pltpu.VMEM((N_HEADS // 2, LANES + ONES_ROWS, 2 * LANES), F32),
            pltpu.VMEM((8, LANES), F32),
        ],
        compiler_params=_params(("parallel", "arbitrary")),
        name="dsa_prompt",
    )(q, qi, wit, kb.reshape(batch, t, D_ATTN), ki2.reshape(batch, t, 2 * IDX_DIM),
      vt.reshape(batch, ntk, D_ATTN, KEY_TILE))


def _conv_tail(y, dwb_ref, lng_ref, lnb_ref, pww_ref):
    y = y + dwb_ref[...]
    mu = jnp.mean(y, axis=-1, keepdims=True)
    var = jnp.mean(jnp.square(y - mu), axis=-1, keepdims=True)
    y = (y - mu) * lax.rsqrt(var + EPS) * lng_ref[...] + lnb_ref[...]
    return _dot((y * _sigmoid(y)).astype(BF16), pww_ref[...])


def _merge(a, cv, yp, x, gate, wo_ref):
    m = _dot(a, wo_ref[0:D_ATTN, :])
    m = m + _dot(cv.astype(BF16), wo_ref[D_ATTN:D_ATTN + C_CONV, :])
    m = m + _dot(yp.astype(BF16), wo_ref[D_ATTN + C_CONV:, :])
    return x + gate * m


def _pool_window_lanes(shape):
    grp = lax.broadcasted_iota(I32, shape, len(shape) - 1) // POOL_GROUP
    w = jnp.full(shape, POOL_WINDOWS[-1], I32)
    for gi in range(len(POOL_WINDOWS) - 2, -1, -1):
        w = jnp.where(grp == gi, POOL_WINDOWS[gi], w)
    return grp, w


def _mix_kernel(a_ref, u_ref, uh_ref, p_ref, ph_ref, x_ref, gate_ref, dww_ref, dwb_ref, lng_ref, lnb_ref,
                pww_ref, pbd_ref, psc_ref, wo_ref, o_ref, ub_ref, pb_ref, pc_ref, *, tiles_per_seq):
    tm = u_ref.shape[0]
    i = pl.program_id(0)
    ti = i % tiles_per_seq
    first = ti == 0
    pad = 16

    ub_ref[0:HALO, :] = jnp.where(first, 0.0, uh_ref[...])
    ub_ref[HALO:HALO + tm, :] = u_ref[...]
    y = jnp.zeros((tm, C_CONV), F32)
    for tap in range(CONV_WIDTH):
        y = y + dww_ref[tap:tap + 1, :] * ub_ref[pl.ds(HALO - (CONV_WIDTH - 1) + tap, tm), :]
    cv = _conv_tail(y, dwb_ref, lng_ref, lnb_ref, pww_ref)

    n = HALO + tm
    pb_ref[0:pad, :] = jnp.zeros((pad, C_POOL), F32)
    pc_ref[0:pad, :] = jnp.zeros((pad, C_POOL), F32)
    pb_ref[pad:pad + HALO, :] = jnp.where(first, 0.0, ph_ref[...])
    pb_ref[pad + HALO:pad + n, :] = p_ref[...]
    cur = p_ref[...]
    grp, wl = _pool_window_lanes((tm, C_POOL))
    src, dst = pb_ref, pc_ref
    win = jnp.zeros((tm, C_POOL), F32)
    for gi, w in enumerate(POOL_WINDOWS):
        s = w // 2
        dst[pad:pad + n, :] = src[pl.ds(pad, n), :] + src[pl.ds(pad - s, n), :]
        win = jnp.where(grp == gi, dst[pl.ds(pad + HALO, tm), :], win)
        src, dst = dst, src
    pos = ti * tm + lax.broadcasted_iota(I32, (tm, C_POOL), 0)
    cnt = jnp.minimum(pos + 1, wl).astype(F32)
    d = win / cnt - cur
    yp = _dot(d.astype(BF16), pbd_ref[...]) * psc_ref[...]

    o_ref[...] = _merge(a_ref[...], cv, yp, x_ref[...], gate_ref[...], wo_ref)


def _mix(a, u, p, x, gate, dww, dwb, lng, lnb, pww, pbd, psc, wo, l, batch):
    n, d = x.shape
    tm = TOKEN_TILE
    tps = (n // batch) // tm
    hb = tm // HALO
    halo = lambda w: pl.BlockSpec((HALO, w), lambda i: (jnp.maximum(i * hb - 1, 0), 0))
    lay = lambda *s: pl.BlockSpec((None,) + s, lambda i: (l,) + (0,) * len(s))
    return pl.pallas_call(
        functools.partial(_mix_kernel, tiles_per_seq=tps),
        out_shape=jax.ShapeDtypeStruct((n, d), F32),
        grid=(n // tm,),
        in_specs=[
            pl.BlockSpec((tm, D_ATTN), lambda i: (i, 0)),
            pl.BlockSpec((tm, C_CONV), lambda i: (i, 0)), halo(C_CONV),
            pl.BlockSpec((tm, C_POOL), lambda i: (i, 0)), halo(C_POOL),
            pl.BlockSpec((tm, d), lambda i: (i, 0)),
            _mod_spec(gate, tps),
            lay(CONV_WIDTH, C_CONV), lay(1, C_CONV), lay(1, C_CONV), lay(1, C_CONV),
            lay(C_CONV, C_CONV), lay(C_POOL, C_POOL), lay(1, C_POOL), lay(d, d),
        ],
        out_specs=pl.BlockSpec((tm, d), lambda i: (i, 0)),
        scratch_shapes=[
            pltpu.VMEM((HALO + tm, C_CONV), F32),
            pltpu.VMEM((16 + HALO + tm, C_POOL), F32),
            pltpu.VMEM((16 + HALO + tm, C_POOL), F32),
        ],
        compiler_params=_params(("parallel",)),
        name="mix",
    )(a, u, u, p, p, x, gate, dww, dwb, lng, lnb, pww, pbd, psc, wo)


def _sample_scores_kernel(pt_ref, qi_ref, wi_ref, kin_ref, *rest):
    pages, o_ref = rest[:-1], rest[-1]
    qi = qi_ref[...]
    w = wi_ref[...]
    tiles = []
    for pg in pages:
        d = _dot(qi, pg[...].astype(BF16))
        tiles.append(jnp.sum(jnp.maximum(d, 0.0) * w, axis=0, keepdims=True))
    kin = kin_ref[...].astype(BF16).astype(F32)
    dn = jnp.sum(qi.astype(F32) * kin, axis=1, keepdims=True)
    sn = jnp.sum(jnp.maximum(dn, 0.0) * w, axis=0, keepdims=True)
    lane = lax.broadcasted_iota(I32, (1, LANES), 1)
    tiles.append(jnp.where(lane == 0, sn, -jnp.inf))
    o_ref[...] = jnp.concatenate(tiles, axis=1)


def _page_specs(n_pages, width, l):
    return [pl.BlockSpec((None, None, width, PAGE_SIZE), functools.partial(lambda n, pt, jj: (l, pt[n, jj], 0, 0), jj=jj))
            for jj in range(n_pages)]


def _sample_scores(page_table, qi, wit, kif, cache_ki, l):
    n, n_pages = page_table.shape
    width = n_pages * PAGE_SIZE + LANES
    per_seq = lambda *s: pl.BlockSpec((None,) + s, lambda i, pt: (i,) + (0,) * len(s))
    return pl.pallas_call(
        _sample_scores_kernel,
        out_shape=jax.ShapeDtypeStruct((n, 1, width), F32),
        grid_spec=pltpu.PrefetchScalarGridSpec(
            num_scalar_prefetch=1,
            grid=(n,),
            in_specs=[per_seq(IDX_HEADS, IDX_DIM), per_seq(IDX_HEADS, 1), per_seq(1, IDX_DIM)]
            + _page_specs(n_pages, IDX_DIM, l),
            out_specs=per_seq(1, width),
        ),
        compiler_params=_params(("arbitrary",)),
        name="sample_scores",
    )(page_table, qi.reshape(n, IDX_HEADS, IDX_DIM), wit.T.reshape(n, IDX_HEADS, 1),
      kif.reshape(n, 1, IDX_DIM), *([cache_ki] * n_pages))


def _sample_select_kernel(s_ref, o_ref, *, topk, idx_bits):
    s = s_ref[...]
    idx = lax.broadcasted_iota(I32, s.shape, 1)
    shape = (s.shape[0], 1)

    def count(x):
        return jnp.sum(x, axis=1, keepdims=True)

    thr, _ = _kth_largest(lambda t: count(jnp.where(s >= t, 1.0, 0.0)), shape, topk)
    eq = s == thr
    need = topk - count(jnp.where(s > thr, 1.0, 0.0))
    cut = _tie_cut(lambda x: count(jnp.where(eq, jnp.where(idx < x, 1.0, 0.0), 0.0)), need, shape, idx_bits)
    o_ref[...] = _select_bias(s, idx, thr, cut)


def _sample_select(scores, total):
    n, width = scores.shape
    topk = min(TOPK_MAX, total // 4)
    return pl.pallas_call(
        functools.partial(_sample_select_kernel, topk=topk, idx_bits=max(1, math.ceil(math.log2(width)))),
        out_shape=jax.ShapeDtypeStruct((n, width), F32),
        name="sample_select",
    )(scores)


def _sample_attn_kernel(pt_ref, q_ref, kn_ref, vn_ref, b_ref, *rest):
    n_pages = (len(rest) - 1) // 2
    kp, vp, o_ref = rest[:n_pages], rest[n_pages:2 * n_pages], rest[-1]
    past = n_pages * PAGE_SIZE
    own = (lax.broadcasted_iota(I32, (N_HEADS, D_ATTN), 0)
           == lax.broadcasted_iota(I32, (N_HEADS, D_ATTN), 1) // HEAD_DIM)
    qf = jnp.where(own, jnp.broadcast_to(q_ref[...].astype(F32), (N_HEADS, D_ATTN)), 0.0)
    qm = qf.astype(BF16)
    bias = b_ref[...]
    s = jnp.concatenate([_dot(qm, pg[...].astype(BF16)) for pg in kp], axis=1) + bias[:, :past]
    kn = kn_ref[...].astype(BF16).astype(F32)
    sn = jnp.sum(qf * kn, axis=1, keepdims=True) + bias[:, past:past + 1]
    m = jnp.maximum(jnp.max(s, axis=1, keepdims=True), sn)
    e = jnp.exp2(s - m)
    en = jnp.exp2(sn - m)
    den = jnp.sum(e, axis=1, keepdims=True) + en
    o = en * vn_ref[...].astype(BF16).astype(F32)
    eb = e.astype(BF16)
    for jj, pg in enumerate(vp):
        o = o + _dot_nt(eb[:, jj * PAGE_SIZE:(jj + 1) * PAGE_SIZE], pg[...].astype(BF16))
    o = o / den
    o_ref[...] = jnp.sum(jnp.where(own, o, 0.0), axis=0, keepdims=True).astype(o_ref.dtype)


def _sample_attn(page_table, q, kf, vf, bias, cache_k, cache_v, l):
    n, n_pages = page_table.shape
    width = bias.shape[-1]
    per_seq = lambda *s: pl.BlockSpec((None,) + s, lambda i, pt: (i,) + (0,) * len(s))
    return pl.pallas_call(
        _sample_attn_kernel,
        out_shape=jax.ShapeDtypeStruct((n, 1, D_ATTN), BF16),
        grid_spec=pltpu.PrefetchScalarGridSpec(
            num_scalar_prefetch=1,
            grid=(n,),
            in_specs=[per_seq(1, D_ATTN), per_seq(1, D_ATTN), per_seq(1, D_ATTN), per_seq(1, width)]
            + _page_specs(n_pages, D_ATTN, l) + _page_specs(n_pages, D_ATTN, l),
            out_specs=per_seq(1, D_ATTN),
        ),
        compiler_params=_params(("arbitrary",)),
        name="sample_attn",
    )(page_table, q.reshape(n, 1, D_ATTN), kf.reshape(n, 1, D_ATTN), vf.reshape(n, 1, D_ATTN),
      bias.reshape(n, 1, width), *([cache_k] * n_pages), *([cache_v] * n_pages))


def _sample_mix_kernel(a_ref, u_ref, sc_ref, p_ref, sp_ref, x_ref, gate_ref, dww_ref, dwb_ref, lng_ref, lnb_ref,
                       pww_ref, pbd_ref, psc_ref, wo_ref, o_ref, *, pos):
    u = u_ref[...]
    hist = CONV_WIDTH - 1
    y = dww_ref[hist:hist + 1, :] * u
    for tap in range(hist):
        y = y + dww_ref[tap:tap + 1, :] * sc_ref[tap]
    cv = _conv_tail(y, dwb_ref, lng_ref, lnb_ref, pww_ref)

    cur = p_ref[...]
    grp, wl = _pool_window_lanes(cur.shape)
    run = cur
    win = jnp.zeros_like(cur)
    back = 0
    for gi, w in enumerate(POOL_WINDOWS):
        while back < w - 1:
            back += 1
            run = run + sp_ref[POOL_HIST - back]
        win = jnp.where(grp == gi, run, win)
    cnt = jnp.minimum(pos + 1, wl).astype(F32)
    d = win / cnt - cur
    yp = _dot(d.astype(BF16), pbd_ref[...]) * psc_ref[...]

    o_ref[...] = _merge(a_ref[...], cv, yp, x_ref[...], gate_ref[...], wo_ref)


def _sample_mix(a, u, state_conv, p, state_pool, x, gate, dww, dwb, lng, lnb, pww, pbd, psc, wo, l, pos):
    n, d = x.shape
    full = lambda arr: pl.BlockSpec(arr.shape, lambda i: (0,) * arr.ndim)
    lay = lambda *s: pl.BlockSpec((None,) + s, lambda i: (l,) + (0,) * len(s))
    sc = jnp.swapaxes(state_conv, 1, 2)
    sp = jnp.swapaxes(state_pool, 1, 2)
    return pl.pallas_call(
        functools.partial(_sample_mix_kernel, pos=pos),
        out_shape=jax.ShapeDtypeStruct((n, d), F32),
        grid=(1,),
        in_specs=[
            full(a), full(u), lay(CONV_WIDTH - 1, n, C_CONV), full(p), lay(POOL_HIST, n, C_POOL), full(x),
            pl.BlockSpec((None, n, d), lambda i: (0, 0, 0)),
            lay(CONV_WIDTH, C_CONV), lay(1, C_CONV), lay(1, C_CONV), lay(1, C_CONV),
            lay(C_CONV, C_CONV), lay(C_POOL, C_POOL), lay(1, C_POOL), lay(d, d),
        ],
        out_specs=full(x),
        compiler_params=_params(("arbitrary",)),
        name="sample_mix",
    )(a, u, sc, p, sp, x, gate, dww, dwb, lng, lnb, pww, pbd, psc, wo)


def _final_norm_kernel(x_ref, g_ref, o_ref):
    x = x_ref[...]
    o_ref[...] = x * lax.rsqrt(jnp.mean(x * x, axis=-1, keepdims=True) + EPS) * g_ref[...]


def _final_norm(x, g):
    n, d = x.shape
    tm = min(TOKEN_TILE, n)
    return pl.pallas_call(
        _final_norm_kernel,
        out_shape=jax.ShapeDtypeStruct((n, d), F32),
        grid=(n // tm,),
        in_specs=[pl.BlockSpec((tm, d), lambda i: (i, 0)), pl.BlockSpec((1, d), lambda i: (0, 0))],
        out_specs=pl.BlockSpec((tm, d), lambda i: (i, 0)),
        compiler_params=_params(("parallel",)),
        name="final_norm",
    )(x, g)


def _prep_weights(w_in, w_out, conv_pw_w, pool_w, ffn_wg, ffn_wu, ffn_wd):
    ki = w_in[:, :, OFF_KI:OFF_WI]
    wcat = jnp.concatenate([
        w_in[:, :, OFF_Q:OFF_V], w_in[:, :, OFF_QI:OFF_KI], ki, ki,
        w_in[:, :, OFF_CONV:OFF_CONV + C_CONV], w_in[:, :, OFF_CONV + C_CONV:OFF_POOL], w_in[:, :, OFF_POOL:D_IN],
    ], axis=-1).astype(BF16)
    depth = pool_w.shape[0]
    wt = jnp.swapaxes(jnp.concatenate([
        w_in[:, :, OFF_K:OFF_QI], ki, w_in[:, :, OFF_WI:OFF_CONV],
        jnp.zeros((depth, w_in.shape[1], R_ALL - R_WI - IDX_HEADS), w_in.dtype),
    ], axis=-1), 1, 2).astype(BF16)
    pbd = jnp.zeros((depth, C_POOL, C_POOL), F32)
    for gi in range(len(POOL_WINDOWS)):
        sl = slice(gi * POOL_GROUP, (gi + 1) * POOL_GROUP)
        pbd = pbd.at[:, sl, sl].set(pool_w[:, gi])
    return (wcat, wt, w_out.astype(BF16), conv_pw_w.astype(BF16), pbd.astype(BF16),
            ffn_wg.astype(BF16), ffn_wu.astype(BF16), ffn_wd.astype(BF16))


def kernel(x_prompt, x_sample, cache_k, cache_v, cache_ki, state_conv, state_pool, page_table, c_prompt, c_sample, ada_w, ada_b, norm_g, ffn_wg, ffn_wu, ffn_wd, w_in, w_out, conv_dw_w, conv_dw_b, conv_ln_g, conv_ln_b, conv_pw_w, pool_w, pool_scale, final_g):
    batch, seq, d = x_prompt.shape
    nseq = x_sample.shape[0]
    depth = w_in.shape[0]
    n_pool = cache_k.shape[1]
    n_pages = page_table.shape[1]
    past = n_pages * PAGE_SIZE
    assert x_sample.shape[1] == 1 and d == D_MODEL
    assert seq % TOKEN_TILE == 0 and TOKEN_TILE == KEY_TILE and nseq % 8 == 0

    wcat, wt, wo, pww, pbd, wg, wu, wd = _prep_weights(w_in, w_out, conv_pw_w, pool_w, ffn_wg, ffn_wu, ffn_wd)
    ck = jnp.transpose(cache_k, (0, 1, 3, 4, 2)).reshape(depth, n_pool, D_ATTN, PAGE_SIZE)
    cv_ = jnp.transpose(cache_v, (0, 1, 3, 4, 2)).reshape(depth, n_pool, D_ATTN, PAGE_SIZE)
    cki = jnp.transpose(cache_ki, (0, 1, 3, 2))
    dwb = conv_dw_b.reshape(depth, 1, C_CONV)
    lng = conv_ln_g.reshape(depth, 1, C_CONV)
    lnb = conv_ln_b.reshape(depth, 1, C_CONV)
    psc = pool_scale.reshape(depth, 1, C_POOL)

    rows = batch + nseq
    pad = (-rows) % 8
    c_all = jnp.concatenate([c_prompt, c_sample, jnp.zeros((pad, d), F32)], axis=0)
    mods = _ada(c_all, ada_w, ada_b)

    xp = x_prompt.reshape(batch * seq, d)
    xs = x_sample.reshape(nseq, d)
    outs = [[] for _ in range(10)]
    for l in range(depth):
        mp = [mods[l, :batch, i * d:(i + 1) * d].reshape(batch, 1, d) for i in range(N_MOD)]
        ms = [mods[l, batch:rows, i * d:(i + 1) * d].reshape(1, nseq, d) for i in range(N_MOD)]
        g = [norm_g[l, i].reshape(1, d) for i in range(3)]
        mixw = (conv_dw_w, dwb, lng, lnb, pww, pbd, psc, wo, l)

        xp = _ffn(xp, g[0], mp[0], mp[1], mp[2], wg, wu, wd, l, 0)
        xs = _ffn(xs, g[0], ms[0], ms[1], ms[2], wg, wu, wd, l, 0)

        q, kb, ktf, vtf, vt, qi, kitf, ki2, wit, u, p = _inproj(xp, g[1], mp[3], mp[4], wcat, wt, l)
        a = _dsa_prompt(q, qi, wit, kb, ki2, vt, batch)
        xp = _mix(a, u, p, xp, mp[5], *mixw, batch)
        u3 = u.reshape(batch, seq, C_CONV)
        p3 = p.reshape(batch, seq, C_POOL)
        for lst, val in zip(outs[:5], (ktf, vtf, kitf, u3[:, seq - (CONV_WIDTH - 1):], p3[:, seq - POOL_HIST:])):
            lst.append(val)

        q, kb, ktf, vtf, vt, qi, kitf, ki2, wit, u, p = _inproj(xs, g[1], ms[3], ms[4], wcat, wt, l)
        scores = _sample_scores(page_table, qi, wit, kitf[0].T, cki, l)
        bias = _sample_select(scores.reshape(nseq, -1), past + 1)
        a = _sample_attn(page_table, q, ktf[0].T, vtf[0].T, bias, ck, cv_, l).reshape(nseq, D_ATTN)
        xs = _sample_mix(a, u, state_conv, p, state_pool, xs, ms[5], *mixw, past)
        for lst, val in zip(outs[5:], (ktf, vtf, kitf,
                                       jnp.concatenate([state_conv[l][:, 1:], u[:, None, :]], axis=1),
                                       jnp.concatenate([state_pool[l][:, 1:], p[:, None, :]], axis=1))):
            lst.append(val)

        xp = _ffn(xp, g[2], mp[6], mp[7], mp[8], wg, wu, wd, l, 1)
        xs = _ffn(xs, g[2], ms[6], ms[7], ms[8], wg, wu, wd, l, 1)

    fg = final_g.reshape(1, d)
    y_prompt = _final_norm(xp, fg).reshape(batch, seq, d)
    y_sample = _final_norm(xs, fg).reshape(nseq, 1, d)
    st = [jnp.stack(o) for o in outs]

    def heads_last(kt):
        return jnp.transpose(kt.reshape(depth, kt.shape[1], N_HEADS, HEAD_DIM, kt.shape[3]), (0, 1, 4, 2, 3))

    def per_seq(a):
        return jnp.swapaxes(a, 1, 2)

    return (y_prompt, y_sample,
            heads_last(st[0]), heads_last(st[1]), jnp.swapaxes(st[2], 2, 3), st[3], st[4],
            per_seq(heads_last(st[5])), per_seq(heads_last(st[6])), per_seq(jnp.swapaxes(st[7], 2, 3)), st[8], st[9])
```

```python
import functools
import math

import jax
import jax.numpy as jnp
from jax import lax
from jax.experimental import pallas as pl
from jax.experimental.pallas import tpu as pltpu

F32 = jnp.float32
BF16 = jnp.bfloat16
I32 = jnp.int32
I16 = jnp.int16

D_MODEL = 1024
N_HEADS = 8
HEAD_DIM = 64
D_ATTN = N_HEADS * HEAD_DIM
IDX_HEADS = 8
IDX_DIM = 64
TOPK_MAX = 256
PAGE_SIZE = 128
C_CONV = 256
CONV_WIDTH = 31
C_POOL = 256
POOL_WINDOWS = (2, 4, 8, 16)
POOL_GROUP = C_POOL // len(POOL_WINDOWS)
POOL_HIST = max(POOL_WINDOWS) - 1
D_FF = 2816
EPS = 1e-6
N_MOD = 9
OFF_Q = 0
OFF_K = OFF_Q + D_ATTN
OFF_V = OFF_K + D_ATTN
OFF_QI = OFF_V + D_ATTN
OFF_KI = OFF_QI + IDX_HEADS * IDX_DIM
OFF_WI = OFF_KI + IDX_DIM
OFF_CONV = OFF_WI + IDX_HEADS
OFF_POOL = OFF_CONV + 2 * C_CONV
D_IN = OFF_POOL + C_POOL

LANES = 128
Q_TILE = 128
KEY_TILE = 512
TOKEN_TILE = 512
FF_TILE = 1408
HALO = 32
MASKED = -1e30
KEY_NEG_INF = -2139095041
KEY_LOWEST = -2139095040
LIMB_OFFSET = 32768
ROW_FOLD = 64
ONES_ROWS = 16
LOG2E = 1.4426950408889634
SHIFT_SLACK = 64.0
VMEM_LIMIT = 48 * 1024 * 1024

C_Q, C_K, C_QI, C_KI2, C_CA, C_CG, C_P = 0, 512, 1024, 1536, 1664, 1920, 2176
D_CAT = 2432
R_K, R_V, R_KI, R_WI = 0, 512, 1024, 1088
R_ALL = 1104


def _sigmoid(x):
    return 1.0 / (1.0 + jnp.exp(-x))


def _dot(a, b):
    return jnp.dot(a, b, preferred_element_type=F32)


def _dot_nt(a, b):
    return lax.dot_general(a, b, (((1,), (1,)), ((), ())), preferred_element_type=F32)


def _params(sem, vmem=VMEM_LIMIT):
    return pltpu.CompilerParams(dimension_semantics=sem, vmem_limit_bytes=vmem)


def _ada_kernel(c_ref, w_ref, b_ref, o_ref):
    c = c_ref[...]
    a = (c * _sigmoid(c)).astype(BF16)
    o_ref[...] = _dot(a, w_ref[...].astype(BF16)) + b_ref[...]


def _ada(c_all, ada_w, ada_b):
    depth, d, n = ada_w.shape
    r = c_all.shape[0]
    tn = 1024
    return pl.pallas_call(
        _ada_kernel,
        out_shape=jax.ShapeDtypeStruct((depth, r, n), F32),
        grid=(depth, n // tn),
        in_specs=[
            pl.BlockSpec((r, d), lambda l, j: (0, 0)),
            pl.BlockSpec((None, d, tn), lambda l, j: (l, 0, j)),
            pl.BlockSpec((None, 1, tn), lambda l, j: (l, 0, j)),
        ],
        out_specs=pl.BlockSpec((None, r, tn), lambda l, j: (l, 0, j)),
        compiler_params=_params(("parallel", "parallel")),
        name="ada",
    )(c_all, ada_w, ada_b.reshape(depth, 1, n))


def _norm_mod(x, g, shift, scale):
    y = x * lax.rsqrt(jnp.mean(x * x, axis=-1, keepdims=True) + EPS) * g
    return y * (1.0 + scale) + shift


def _ffn_kernel(x_ref, g_ref, sh_ref, sc_ref, gt_ref, wg_ref, wu_ref, wd_ref, o_ref, h_ref, acc_ref):
    j = pl.program_id(1)

    @pl.when(j == 0)
    def _():
        h_ref[...] = _norm_mod(x_ref[...], g_ref[...], sh_ref[...], sc_ref[...]).astype(BF16)

    h = h_ref[...]
    g = _dot(h, wg_ref[...])
    u = _dot(h, wu_ref[...])
    part = _dot(((g * _sigmoid(g)) * u).astype(BF16), wd_ref[...])

    @pl.when(j == 0)
    def _():
        acc_ref[...] = part

    @pl.when(j > 0)
    def _():
        acc_ref[...] += part

    @pl.when(j == pl.num_programs(1) - 1)
    def _():
        o_ref[...] = x_ref[...] + (0.5 * gt_ref[...]) * acc_ref[...]


def _mod_spec(mod, tiles_per_group):
    _, r, d = mod.shape
    return pl.BlockSpec((None, r, d), lambda i, *_: (i // tiles_per_group, 0, 0))


def _ffn(x, g, shift, scale, gate, wg, wu, wd, l, s):
    n, d = x.shape
    tm = min(TOKEN_TILE, n)
    tpg = (n // shift.shape[0]) // tm
    nf = D_FF // FF_TILE
    return pl.pallas_call(
        _ffn_kernel,
        out_shape=jax.ShapeDtypeStruct((n, d), F32),
        grid=(n // tm, nf),
        in_specs=[
            pl.BlockSpec((tm, d), lambda i, j: (i, 0)),
            pl.BlockSpec((1, d), lambda i, j: (0, 0)),
            _mod_spec(shift, tpg), _mod_spec(scale, tpg), _mod_spec(gate, tpg),
            pl.BlockSpec((None, None, d, FF_TILE), lambda i, j: (l, s, 0, j)),
            pl.BlockSpec((None, None, d, FF_TILE), lambda i, j: (l, s, 0, j)),
            pl.BlockSpec((None, None, FF_TILE, d), lambda i, j: (l, s, j, 0)),
        ],
        out_specs=pl.BlockSpec((tm, d), lambda i, j: (i, 0)),
        scratch_shapes=[pltpu.VMEM((tm, d), BF16), pltpu.VMEM((tm, d), F32)],
        compiler_params=_params(("parallel", "arbitrary")),
        name="ffn",
    )(x, g, shift, scale, gate, wg, wu, wd)


def _inproj_kernel(x_ref, g_ref, sh_ref, sc_ref, wcat_ref, wt_ref,
                   q_ref, kb_ref, ktf_ref, vtf_ref, vt_ref, qi_ref, kitf_ref, ki2_ref, wit_ref, u_ref, p_ref):
    h = _norm_mod(x_ref[...], g_ref[...], sh_ref[...], sc_ref[...]).astype(BF16)
    z = _dot(h, wcat_ref[...])
    zt = _dot_nt(wt_ref[...], h)
    q_ref[...] = (z[:, C_Q:C_K] * (HEAD_DIM ** -0.5 * LOG2E)).astype(BF16)
    kb_ref[...] = z[:, C_K:C_QI].astype(BF16)
    ktf_ref[...] = zt[R_K:R_V]
    vt = zt[R_V:R_KI]
    vtf_ref[...] = vt
    vt_ref[...] = vt.astype(BF16)
    qi_ref[...] = z[:, C_QI:C_KI2].astype(BF16)
    ki2_ref[...] = z[:, C_KI2:C_CA].astype(BF16)
    kitf_ref[...] = zt[R_KI:R_WI]
    wit_ref[...] = zt[R_WI:R_WI + IDX_HEADS] * IDX_HEADS ** -0.5 * IDX_DIM ** -0.5
    u_ref[...] = z[:, C_CA:C_CG] * _sigmoid(z[:, C_CG:C_P])
    p_ref[...] = z[:, C_P:D_CAT]


def _inproj(x, g, shift, scale, wcat, wt, l):
    n, d = x.shape
    tm = min(TOKEN_TILE, n)
    groups = shift.shape[0]
    tg = n // groups
    tpg = tg // tm
    nt = n // tm
    row = lambda w: pl.BlockSpec((tm, w), lambda i: (i, 0))
    col = lambda w: pl.BlockSpec((None, w, tm), lambda i: (i // tpg, 0, i % tpg))
    out_shape = (
        jax.ShapeDtypeStruct((n, D_ATTN), BF16),
        jax.ShapeDtypeStruct((n, D_ATTN), BF16),
        jax.ShapeDtypeStruct((groups, D_ATTN, tg), F32),
        jax.ShapeDtypeStruct((groups, D_ATTN, tg), F32),
        jax.ShapeDtypeStruct((nt, D_ATTN, tm), BF16),
        jax.ShapeDtypeStruct((n, D_ATTN), BF16),
        jax.ShapeDtypeStruct((groups, IDX_DIM, tg), F32),
        jax.ShapeDtypeStruct((n, 2 * IDX_DIM), BF16),
        jax.ShapeDtypeStruct((IDX_HEADS, n), F32),
        jax.ShapeDtypeStruct((n, C_CONV), F32),
        jax.ShapeDtypeStruct((n, C_POOL), F32),
    )
    out_specs = (
        row(D_ATTN), row(D_ATTN), col(D_ATTN), col(D_ATTN),
        pl.BlockSpec((None, D_ATTN, tm), lambda i: (i, 0, 0)),
        row(D_ATTN), col(IDX_DIM), row(2 * IDX_DIM),
        pl.BlockSpec((IDX_HEADS, tm), lambda i: (0, i)),
        row(C_CONV), row(C_POOL),
    )
    return pl.pallas_call(
        _inproj_kernel,
        out_shape=out_shape,
        grid=(nt,),
        in_specs=[
            pl.BlockSpec((tm, d), lambda i: (i, 0)),
            pl.BlockSpec((1, d), lambda i: (0, 0)),
            _mod_spec(shift, tpg), _mod_spec(scale, tpg),
            pl.BlockSpec((None, d, D_CAT), lambda i: (l, 0, 0)),
            pl.BlockSpec((None, R_ALL, d), lambda i: (l, 0, 0)),
        ],
        out_specs=out_specs,
        compiler_params=_params(("parallel",)),
        name="inproj",
    )(x, g, shift, scale, wcat, wt)


def _key_to_float(key):
    bits = jnp.where(key < 0, key ^ jnp.int32(-2 ** 31), ~key)
    return lax.bitcast_convert_type(bits, F32)


def _kth_largest(count_ge, shape, k):
    def step(i, carry):
        key, n_ge = carry
        cand = key | jnp.left_shift(jnp.int32(1), jnp.int32(31) - i)
        n = count_ge(_key_to_float(cand))
        take = n >= k
        return jnp.where(take, cand, key), jnp.where(take, n, n_ge)

    key, n_ge = lax.fori_loop(0, 32, step, (jnp.zeros(shape, I32), jnp.full(shape, jnp.inf, F32)))
    return _key_to_float(key), n_ge


def _tie_cut(count_eq_below, need, shape, nbits):
    def step(i, x):
        cand = x | jnp.left_shift(jnp.int32(1), jnp.int32(nbits - 1) - i)
        return jnp.where(count_eq_below(cand) < need, cand, x)

    return lax.fori_loop(0, nbits, step, jnp.zeros(shape, I32))


def _ordered_key(x):
    bits = lax.bitcast_convert_type(x, I32)
    return bits ^ (lax.shift_right_arithmetic(bits, 31) & jnp.int32(0x7FFFFFFF))


def _key_limbs(key):
    hi = lax.shift_right_arithmetic(key, 16)
    lo = lax.shift_right_arithmetic(lax.shift_left(key ^ jnp.int32(0x8000), 16), 16)
    return hi.astype(I16), lo.astype(I16)


def _kth_largest_limb(count_ge, k, shape):
    def step(i, carry):
        prefix, n_ge = carry
        cand = prefix | jnp.left_shift(jnp.int32(1), jnp.int32(15) - i)
        n = count_ge((cand - LIMB_OFFSET).astype(I16))
        take = n >= k
        return jnp.where(take, cand, prefix), jnp.where(take, n, n_ge)

    prefix, n_ge = lax.fori_loop(0, 16, step, (jnp.zeros(shape, I32), jnp.full(shape, 2 ** 30, I32)))
    return prefix - LIMB_OFFSET, n_ge


def _select_bias(s, idx, thr, cut):
    return jnp.where(s > thr, 0.0, jnp.where(s == thr, jnp.where(idx <= cut, 0.0, MASKED), MASKED))


def _stage_head_pairs(src_ref, dst_ref):
    lo = lax.broadcasted_iota(I32, (Q_TILE, LANES), 1) < HEAD_DIM
    for p in range(N_HEADS // 2):
        t = src_ref[:, p * LANES:(p + 1) * LANES]
        z = jnp.zeros_like(t)
        dst_ref[p, 0:Q_TILE, 0:LANES] = jnp.where(lo, t, z)
        dst_ref[p, Q_TILE:2 * Q_TILE, 0:LANES] = jnp.where(lo, z, t)


def _fold_rows(x, op):
    rows, lanes = x.shape
    return op(x.reshape(rows // ROW_FOLD, ROW_FOLD, lanes), axis=0)


def _bf16_exact(x):
    bits = lax.bitcast_convert_type(x, I32) & jnp.int32(-65536)
    return lax.bitcast_convert_type(bits, F32)


def _max_to_tile(x):
    x = _fold_rows(x, jnp.max)
    parts = [x[r:r + 8] for r in range(0, ROW_FOLD, 8)]
    while len(parts) > 1:
        parts = [jnp.maximum(parts[i], parts[i + 1]) for i in range(0, len(parts), 2)]
    return parts[0]


def _dsa_prompt_kernel(q_ref, qi_ref, wit_ref, k_ref, ki2_ref, vt_ref, o_ref,
                       s_ref, hi_ref, lo_ref, qp_ref, qip_ref, acc_ref, peak_ref, thr_ref, *, topk, idx_bits):
    j = pl.program_id(1)
    tk = KEY_TILE
    nch = ((j + 1) * Q_TILE + tk - 1) // tk
    npair = N_HEADS // 2

    _stage_head_pairs(qi_ref, qip_ref)
    _stage_head_pairs(q_ref, qp_ref)
    wt = wit_ref[...]
    wrow = [jnp.concatenate([wt[2 * p:2 * p + 1], wt[2 * p + 1:2 * p + 2]], axis=1) for p in range(npair)]
    qpos = j * Q_TILE + lax.broadcasted_iota(I32, (1, LANES), 1)
    krow = lax.broadcasted_iota(I32, (tk, LANES), 0)

    def tile(c):
        return pl.ds(pl.multiple_of(c * tk, tk), tk)

    def for_tiles(body):
        def two(i, carry):
            body(2 * i)
            body(2 * i + 1)
            return carry

        lax.fori_loop(0, nch // 2, two, 0)

        @pl.when(nch % 2 == 1)
        def _():
            body(nch - 1)

    def scores(c):
        kic = ki2_ref[tile(c), :]
        acc = None
        dots = [_dot_nt(kic, qip_ref[p]) for p in range(npair)]
        for p in range(npair):
            r = jnp.maximum(dots[p], 0.0) * wrow[p]
            r = r[:, :LANES] + r[:, LANES:]
            acc = r if acc is None else acc + r
        key = jnp.where(krow <= qpos - c * tk, _ordered_key(acc), KEY_NEG_INF)
        s_ref[tile(c), :] = key
        hi_ref[tile(c), :], lo_ref[tile(c), :] = _key_limbs(key)

    for_tiles(scores)

    def count(pred):
        def body(c, acc):
            return acc + _fold_rows(pred(s_ref[tile(c), :], c), jnp.sum)
        acc = lax.fori_loop(0, nch, body, jnp.zeros((ROW_FOLD, LANES), F32))
        return jnp.sum(acc, axis=0, keepdims=True)

    def count_limb(ref, t):
        def body(c, acc):
            ge = jnp.where(ref[tile(c), :] >= t, jnp.int16(1), jnp.int16(0))
            parts = [ge[r:r + ROW_FOLD] for r in range(0, tk, ROW_FOLD)]
            while len(parts) > 1:
                parts = [parts[i] + parts[i + 1] for i in range(0, len(parts), 2)]
            return acc + parts[0]
        acc = lax.fori_loop(0, nch, body, jnp.zeros((ROW_FOLD, LANES), I16))
        return jnp.sum(acc.astype(I32), axis=0, keepdims=True)

    thr_ref[...] = jnp.full(thr_ref.shape, KEY_LOWEST, I32)

    @pl.when((j + 1) * Q_TILE > topk)
    def _():
        row = (1, LANES)
        t_hi, _ = _kth_largest_limb(lambda t: count_limb(hi_ref, t), topk, row)
        above = jnp.where(t_hi < LIMB_OFFSET - 1,
                          count_limb(hi_ref, jnp.minimum(t_hi + 1, LIMB_OFFSET - 1).astype(I16)), 0)
        t_hi16 = t_hi.astype(I16)

        def keep_low(c, carry):
            lo_ref[tile(c), :] = jnp.where(hi_ref[tile(c), :] == t_hi16, lo_ref[tile(c), :], jnp.int16(-LIMB_OFFSET))
            return carry

        lax.fori_loop(0, nch, keep_low, 0)
        t_lo, n_ge_lo = _kth_largest_limb(lambda t: count_limb(lo_ref, t), topk - above, row)
        thr = jnp.left_shift(t_hi, 16) | (t_lo + LIMB_OFFSET)
        thr_ref[...] = jnp.broadcast_to(thr, thr_ref.shape)

        @pl.when(jnp.max(above + n_ge_lo) > topk)
        def _():
            need = topk - count(lambda s, c: jnp.where(s > thr, 1.0, 0.0))
            cut = _tie_cut(
                lambda x: count(lambda s, c: jnp.where(s == thr, jnp.where(krow + c * tk < x, 1.0, 0.0), 0.0)),
                need, row, idx_bits)

            def drop(c, carry):
                s = s_ref[tile(c), :]
                s_ref[tile(c), :] = jnp.where(s == thr, jnp.where(krow + c * tk > cut, KEY_NEG_INF, s), s)
                return carry

            lax.fori_loop(0, nch, drop, 0)

    thr = thr_ref[0:1, :]

    lane = lax.broadcasted_iota(I32, (LANES, LANES), 1)
    ones = jnp.ones((ONES_ROWS, tk), BF16)

    def ones_columns(rows):
        return jnp.where(lax.broadcasted_iota(I32, (rows, LANES), 1) < N_HEADS, 1.0, 0.0).astype(BF16)

    def set_shifts(shifts):
        rows = [sh[:, half * LANES:(half + 1) * LANES] for sh in shifts for half in range(2)]
        cols = jnp.concatenate(rows + [jnp.zeros((LANES - len(rows), LANES), F32)], axis=0).T
        for i in range(len(rows)):
            qp_ref[i // 2, (i % 2) * Q_TILE:(i % 2 + 1) * Q_TILE, LANES:2 * LANES] = (
                jnp.where(lane == i, -cols, 0.0).astype(BF16))

    def shifted_scores(rows, p, bias):
        k_aug = jnp.concatenate([k_ref[rows, p * LANES:(p + 1) * LANES], ones_columns(bias.shape[0])], axis=1)
        return _dot_nt(k_aug, qp_ref[p]) + bias

    def attend_pass():
        acc_ref[...] = jnp.zeros(acc_ref.shape, F32)
        peak_ref[...] = jnp.full(peak_ref.shape, MASKED, F32)

        def attend(c):
            bias = jnp.where(s_ref[tile(c), :] >= thr, 0.0, MASKED)
            bias = jnp.concatenate([bias, bias], axis=1)
            ss = [shifted_scores(tile(c), p, bias) for p in range(npair)]
            for p in range(npair):
                peak_ref[p] = jnp.maximum(peak_ref[p], _max_to_tile(ss[p]))
                vt1 = jnp.concatenate([vt_ref[c, p * LANES:(p + 1) * LANES, :], ones], axis=0)
                acc_ref[p] += _dot(vt1, jnp.exp2(ss[p]).astype(BF16))

        for_tiles(attend)
        return [jnp.max(peak_ref[p], axis=0, keepdims=True) for p in range(npair)]

    bias0 = jnp.where(s_ref[0:LANES, :] > KEY_NEG_INF, 0.0, MASKED)
    bias0 = jnp.concatenate([bias0, bias0], axis=1)
    shifts = []
    for p in range(npair):
        qp_ref[p, :, LANES:2 * LANES] = jnp.zeros((2 * Q_TILE, LANES), BF16)
        sh = jnp.max(shifted_scores(pl.ds(0, LANES), p, bias0), axis=0, keepdims=True)
        shifts.append(_bf16_exact(sh))
    set_shifts(shifts)
    peaks = attend_pass()

    worst = jnp.abs(peaks[0])
    for p in range(1, npair):
        worst = jnp.maximum(worst, jnp.abs(peaks[p]))

    @pl.when(jnp.max(worst) > SHIFT_SLACK)
    def _():
        set_shifts([_bf16_exact(shifts[p] + peaks[p]) for p in range(npair)])
        attend_pass()

    even = lax.broadcasted_iota(I32, (LANES, LANES), 0) < HEAD_DIM
    for p in range(npair):
        o = acc_ref[p, 0:LANES, :] / acc_ref[p, LANES:LANES + 1, :]
        ot = jnp.where(even, o[:, :LANES], o[:, LANES:])
        o_ref[:, p * LANES:(p + 1) * LANES] = ot.T.astype(o_ref.dtype)


def _dsa_prompt(q, qi, wit, kb, ki2, vt, batch):
    n = q.shape[0]
    t = n // batch
    nq = t // Q_TILE
    ntk = t // KEY_TILE
    topk = min(TOPK_MAX, t // 4)
    assert topk % Q_TILE == 0
    idx_bits = max(1, math.ceil(math.log2(t)))
    once = dict(pipeline_mode=pl.Buffered(1))
    return pl.pallas_call(
        functools.partial(_dsa_prompt_kernel, topk=topk, idx_bits=idx_bits),
        out_shape=jax.ShapeDtypeStruct((n, D_ATTN), BF16),
        grid=(batch, nq),
        in_specs=[
            pl.BlockSpec((Q_TILE, D_ATTN), lambda b, j: (b * nq + j, 0)),
            pl.BlockSpec((Q_TILE, D_ATTN), lambda b, j: (b * nq + j, 0)),
            pl.BlockSpec((IDX_HEADS, Q_TILE), lambda b, j: (0, b * nq + j)),
            pl.BlockSpec((None, t, D_ATTN), lambda b, j: (b, 0, 0), **once),
            pl.BlockSpec((None, t, 2 * IDX_DIM), lambda b, j: (b, 0, 0), **once),
            pl.BlockSpec((None, ntk, D_ATTN, KEY_TILE), lambda b, j: (b, 0, 0, 0), **once),
        ],
        out_specs=pl.BlockSpec((Q_TILE, D_ATTN), lambda b, j: (b * nq + j, 0)),
        scratch_shapes=[
            pltpu.VMEM((t, LANES), I32),
            pltpu.VMEM((t, LANES), I16),
            pltpu.VMEM((t, LANES), I16),
            pltpu.VMEM((N_HEADS // 2, 2 * Q_TILE, 2 * LANES), BF16),
            pltpu.VMEM((N_HEADS // 2, 2 * Q_TILE, LANES), BF16),
            pltpu.VMEM((N_HEADS // 2, LANES + ONES_ROWS, 2 * LANES), F32),
            pltpu.VMEM((N_HEADS // 2, 8, 2 * LANES), F32),
            pltpu.VMEM((8, LANES), I32),
        ],
        compiler_params=_params(("parallel", "arbitrary")),
        name="dsa_prompt",
    )(q, qi, wit, kb.reshape(batch, t, D_ATTN), ki2.reshape(batch, t, 2 * IDX_DIM),
      vt.reshape(batch, ntk, D_ATTN, KEY_TILE))


def _conv_tail(y, dwb_ref, lng_ref, lnb_ref, pww_ref):
    y = y + dwb_ref[...]
    mu = jnp.mean(y, axis=-1, keepdims=True)
    var = jnp.mean(jnp.square(y - mu), axis=-1, keepdims=True)
    y = (y - mu) * lax.rsqrt(var + EPS) * lng_ref[...] + lnb_ref[...]
    return _dot((y * _sigmoid(y)).astype(BF16), pww_ref[...])


def _merge(a, cv, yp, x, gate, wo_ref):
    m = _dot(a, wo_ref[0:D_ATTN, :])
    m = m + _dot(cv.astype(BF16), wo_ref[D_ATTN:D_ATTN + C_CONV, :])
    m = m + _dot(yp.astype(BF16), wo_ref[D_ATTN + C_CONV:, :])
    return x + gate * m


def _pool_window_lanes(shape):
    grp = lax.broadcasted_iota(I32, shape, len(shape) - 1) // POOL_GROUP
    w = jnp.full(shape, POOL_WINDOWS[-1], I32)
    for gi in range(len(POOL_WINDOWS) - 2, -1, -1):
        w = jnp.where(grp == gi, POOL_WINDOWS[gi], w)
    return grp, w


def _mix_kernel(a_ref, u_ref, uh_ref, p_ref, ph_ref, x_ref, gate_ref, dww_ref, dwb_ref, lng_ref, lnb_ref,
                pww_ref, pbd_ref, psc_ref, wo_ref, o_ref, ub_ref, pb_ref, pc_ref, *, tiles_per_seq):
    tm = u_ref.shape[0]
    i = pl.program_id(0)
    ti = i % tiles_per_seq
    first = ti == 0
    pad = 16

    ub_ref[0:HALO, :] = jnp.where(first, 0.0, uh_ref[...])
    ub_ref[HALO:HALO + tm, :] = u_ref[...]
    y = jnp.zeros((tm, C_CONV), F32)
    for tap in range(CONV_WIDTH):
        y = y + dww_ref[tap:tap + 1, :] * ub_ref[pl.ds(HALO - (CONV_WIDTH - 1) + tap, tm), :]
    cv = _conv_tail(y, dwb_ref, lng_ref, lnb_ref, pww_ref)

    n = HALO + tm
    pb_ref[0:pad, :] = jnp.zeros((pad, C_POOL), F32)
    pc_ref[0:pad, :] = jnp.zeros((pad, C_POOL), F32)
    pb_ref[pad:pad + HALO, :] = jnp.where(first, 0.0, ph_ref[...])
    pb_ref[pad + HALO:pad + n, :] = p_ref[...]
    cur = p_ref[...]
    grp, wl = _pool_window_lanes((tm, C_POOL))
    src, dst = pb_ref, pc_ref
    win = jnp.zeros((tm, C_POOL), F32)
    for gi, w in enumerate(POOL_WINDOWS):
        s = w // 2
        dst[pad:pad + n, :] = src[pl.ds(pad, n), :] + src[pl.ds(pad - s, n), :]
        win = jnp.where(grp == gi, dst[pl.ds(pad + HALO, tm), :], win)
        src, dst = dst, src
    pos = ti * tm + lax.broadcasted_iota(I32, (tm, C_POOL), 0)
    cnt = jnp.minimum(pos + 1, wl).astype(F32)
    d = win / cnt - cur
    yp = _dot(d.astype(BF16), pbd_ref[...]) * psc_ref[...]

    o_ref[...] = _merge(a_ref[...], cv, yp, x_ref[...], gate_ref[...], wo_ref)


def _mix(a, u, p, x, gate, dww, dwb, lng, lnb, pww, pbd, psc, wo, l, batch):
    n, d = x.shape
    tm = TOKEN_TILE
    tps = (n // batch) // tm
    hb = tm // HALO
    halo = lambda w: pl.BlockSpec((HALO, w), lambda i: (jnp.maximum(i * hb - 1, 0), 0))
    lay = lambda *s: pl.BlockSpec((None,) + s, lambda i: (l,) + (0,) * len(s))
    return pl.pallas_call(
        functools.partial(_mix_kernel, tiles_per_seq=tps),
        out_shape=jax.ShapeDtypeStruct((n, d), F32),
        grid=(n // tm,),
        in_specs=[
            pl.BlockSpec((tm, D_ATTN), lambda i: (i, 0)),
            pl.BlockSpec((tm, C_CONV), lambda i: (i, 0)), halo(C_CONV),
            pl.BlockSpec((tm, C_POOL), lambda i: (i, 0)), halo(C_POOL),
            pl.BlockSpec((tm, d), lambda i: (i, 0)),
            _mod_spec(gate, tps),
            lay(CONV_WIDTH, C_CONV), lay(1, C_CONV), lay(1, C_CONV), lay(1, C_CONV),
            lay(C_CONV, C_CONV), lay(C_POOL, C_POOL), lay(1, C_POOL), lay(d, d),
        ],
        out_specs=pl.BlockSpec((tm, d), lambda i: (i, 0)),
        scratch_shapes=[
            pltpu.VMEM((HALO + tm, C_CONV), F32),
            pltpu.VMEM((16 + HALO + tm, C_POOL), F32),
            pltpu.VMEM((16 + HALO + tm, C_POOL), F32),
        ],
        compiler_params=_params(("parallel",)),
        name="mix",
    )(a, u, u, p, p, x, gate, dww, dwb, lng, lnb, pww, pbd, psc, wo)


def _sample_scores_kernel(pt_ref, qi_ref, wi_ref, kin_ref, *rest):
    pages, o_ref = rest[:-1], rest[-1]
    qi = qi_ref[...]
    w = wi_ref[...]
    tiles = []
    for pg in pages:
        d = _dot(qi, pg[...].astype(BF16))
        tiles.append(jnp.sum(jnp.maximum(d, 0.0) * w, axis=0, keepdims=True))
    kin = kin_ref[...].astype(BF16).astype(F32)
    dn = jnp.sum(qi.astype(F32) * kin, axis=1, keepdims=True)
    sn = jnp.sum(jnp.maximum(dn, 0.0) * w, axis=0, keepdims=True)
    lane = lax.broadcasted_iota(I32, (1, LANES), 1)
    tiles.append(jnp.where(lane == 0, sn, -jnp.inf))
    o_ref[...] = jnp.concatenate(tiles, axis=1)


def _page_specs(n_pages, width, l):
    return [pl.BlockSpec((None, None, width, PAGE_SIZE), functools.partial(lambda n, pt, jj: (l, pt[n, jj], 0, 0), jj=jj))
            for jj in range(n_pages)]


def _sample_scores(page_table, qi, wit, kif, cache_ki, l):
    n, n_pages = page_table.shape
    width = n_pages * PAGE_SIZE + LANES
    per_seq = lambda *s: pl.BlockSpec((None,) + s, lambda i, pt: (i,) + (0,) * len(s))
    return pl.pallas_call(
        _sample_scores_kernel,
        out_shape=jax.ShapeDtypeStruct((n, 1, width), F32),
        grid_spec=pltpu.PrefetchScalarGridSpec(
            num_scalar_prefetch=1,
            grid=(n,),
            in_specs=[per_seq(IDX_HEADS, IDX_DIM), per_seq(IDX_HEADS, 1), per_seq(1, IDX_DIM)]
            + _page_specs(n_pages, IDX_DIM, l),
            out_specs=per_seq(1, width),
        ),
        compiler_params=_params(("arbitrary",)),
        name="sample_scores",
    )(page_table, qi.reshape(n, IDX_HEADS, IDX_DIM), wit.T.reshape(n, IDX_HEADS, 1),
      kif.reshape(n, 1, IDX_DIM), *([cache_ki] * n_pages))


def _sample_select_kernel(s_ref, o_ref, *, topk, idx_bits):
    s = s_ref[...]
    idx = lax.broadcasted_iota(I32, s.shape, 1)
    shape = (s.shape[0], 1)

    def count(x):
        return jnp.sum(x, axis=1, keepdims=True)

    thr, _ = _kth_largest(lambda t: count(jnp.where(s >= t, 1.0, 0.0)), shape, topk)
    eq = s == thr
    need = topk - count(jnp.where(s > thr, 1.0, 0.0))
    cut = _tie_cut(lambda x: count(jnp.where(eq, jnp.where(idx < x, 1.0, 0.0), 0.0)), need, shape, idx_bits)
    o_ref[...] = _select_bias(s, idx, thr, cut)


def _sample_select(scores, total):
    n, width = scores.shape
    topk = min(TOPK_MAX, total // 4)
    return pl.pallas_call(
        functools.partial(_sample_select_kernel, topk=topk, idx_bits=max(1, math.ceil(math.log2(width)))),
        out_shape=jax.ShapeDtypeStruct((n, width), F32),
        name="sample_select",
    )(scores)


def _sample_attn_kernel(pt_ref, q_ref, kn_ref, vn_ref, b_ref, *rest):
    n_pages = (len(rest) - 1) // 2
    kp, vp, o_ref = rest[:n_pages], rest[n_pages:2 * n_pages], rest[-1]
    past = n_pages * PAGE_SIZE
    own = (lax.broadcasted_iota(I32, (N_HEADS, D_ATTN), 0)
           == lax.broadcasted_iota(I32, (N_HEADS, D_ATTN), 1) // HEAD_DIM)
    qf = jnp.where(own, jnp.broadcast_to(q_ref[...].astype(F32), (N_HEADS, D_ATTN)), 0.0)
    qm = qf.astype(BF16)
    bias = b_ref[...]
    s = jnp.concatenate([_dot(qm, pg[...].astype(BF16)) for pg in kp], axis=1) + bias[:, :past]
    kn = kn_ref[...].astype(BF16).astype(F32)
    sn = jnp.sum(qf * kn, axis=1, keepdims=True) + bias[:, past:past + 1]
    m = jnp.maximum(jnp.max(s, axis=1, keepdims=True), sn)
    e = jnp.exp2(s - m)
    en = jnp.exp2(sn - m)
    den = jnp.sum(e, axis=1, keepdims=True) + en
    o = en * vn_ref[...].astype(BF16).astype(F32)
    eb = e.astype(BF16)
    for jj, pg in enumerate(vp):
        o = o + _dot_nt(eb[:, jj * PAGE_SIZE:(jj + 1) * PAGE_SIZE], pg[...].astype(BF16))
    o = o / den
    o_ref[...] = jnp.sum(jnp.where(own, o, 0.0), axis=0, keepdims=True).astype(o_ref.dtype)


def _sample_attn(page_table, q, kf, vf, bias, cache_k, cache_v, l):
    n, n_pages = page_table.shape
    width = bias.shape[-1]
    per_seq = lambda *s: pl.BlockSpec((None,) + s, lambda i, pt: (i,) + (0,) * len(s))
    return pl.pallas_call(
        _sample_attn_kernel,
        out_shape=jax.ShapeDtypeStruct((n, 1, D_ATTN), BF16),
        grid_spec=pltpu.PrefetchScalarGridSpec(
            num_scalar_prefetch=1,
            grid=(n,),
            in_specs=[per_seq(1, D_ATTN), per_seq(1, D_ATTN), per_seq(1, D_ATTN), per_seq(1, width)]
            + _page_specs(n_pages, D_ATTN, l) + _page_specs(n_pages, D_ATTN, l),
            out_specs=per_seq(1, D_ATTN),
        ),
        compiler_params=_params(("arbitrary",)),
        name="sample_attn",
    )(page_table, q.reshape(n, 1, D_ATTN), kf.reshape(n, 1, D_ATTN), vf.reshape(n, 1, D_ATTN),
      bias.reshape(n, 1, width), *([cache_k] * n_pages), *([cache_v] * n_pages))


def _sample_mix_kernel(a_ref, u_ref, sc_ref, p_ref, sp_ref, x_ref, gate_ref, dww_ref, dwb_ref, lng_ref, lnb_ref,
                       pww_ref, pbd_ref, psc_ref, wo_ref, o_ref, *, pos):
    u = u_ref[...]
    hist = CONV_WIDTH - 1
    y = dww_ref[hist:hist + 1, :] * u
    for tap in range(hist):
        y = y + dww_ref[tap:tap + 1, :] * sc_ref[tap]
    cv = _conv_tail(y, dwb_ref, lng_ref, lnb_ref, pww_ref)

    cur = p_ref[...]
    grp, wl = _pool_window_lanes(cur.shape)
    run = cur
    win = jnp.zeros_like(cur)
    back = 0
    for gi, w in enumerate(POOL_WINDOWS):
        while back < w - 1:
            back += 1
            run = run + sp_ref[POOL_HIST - back]
        win = jnp.where(grp == gi, run, win)
    cnt = jnp.minimum(pos + 1, wl).astype(F32)
    d = win / cnt - cur
    yp = _dot(d.astype(BF16), pbd_ref[...]) * psc_ref[...]

    o_ref[...] = _merge(a_ref[...], cv, yp, x_ref[...], gate_ref[...], wo_ref)


def _sample_mix(a, u, state_conv, p, state_pool, x, gate, dww, dwb, lng, lnb, pww, pbd, psc, wo, l, pos):
    n, d = x.shape
    full = lambda arr: pl.BlockSpec(arr.shape, lambda i: (0,) * arr.ndim)
    lay = lambda *s: pl.BlockSpec((None,) + s, lambda i: (l,) + (0,) * len(s))
    sc = jnp.swapaxes(state_conv, 1, 2)
    sp = jnp.swapaxes(state_pool, 1, 2)
    return pl.pallas_call(
        functools.partial(_sample_mix_kernel, pos=pos),
        out_shape=jax.ShapeDtypeStruct((n, d), F32),
        grid=(1,),
        in_specs=[
            full(a), full(u), lay(CONV_WIDTH - 1, n, C_CONV), full(p), lay(POOL_HIST, n, C_POOL), full(x),
            pl.BlockSpec((None, n, d), lambda i: (0, 0, 0)),
            lay(CONV_WIDTH, C_CONV), lay(1, C_CONV), lay(1, C_CONV), lay(1, C_CONV),
            lay(C_CONV, C_CONV), lay(C_POOL, C_POOL), lay(1, C_POOL), lay(d, d),
        ],
        out_specs=full(x),
        compiler_params=_params(("arbitrary",)),
        name="sample_mix",
    )(a, u, sc, p, sp, x, gate, dww, dwb, lng, lnb, pww, pbd, psc, wo)


def _final_norm_kernel(x_ref, g_ref, o_ref):
    x = x_ref[...]
    o_ref[...] = x * lax.rsqrt(jnp.mean(x * x, axis=-1, keepdims=True) + EPS) * g_ref[...]


def _final_norm(x, g):
    n, d = x.shape
    tm = min(TOKEN_TILE, n)
    return pl.pallas_call(
        _final_norm_kernel,
        out_shape=jax.ShapeDtypeStruct((n, d), F32),
        grid=(n // tm,),
        in_specs=[pl.BlockSpec((tm, d), lambda i: (i, 0)), pl.BlockSpec((1, d), lambda i: (0, 0))],
        out_specs=pl.BlockSpec((tm, d), lambda i: (i, 0)),
        compiler_params=_params(("parallel",)),
        name="final_norm",
    )(x, g)


def _prep_weights(w_in, w_out, conv_pw_w, pool_w, ffn_wg, ffn_wu, ffn_wd):
    ki = w_in[:, :, OFF_KI:OFF_WI]
    wcat = jnp.concatenate([
        w_in[:, :, OFF_Q:OFF_V], w_in[:, :, OFF_QI:OFF_KI], ki, ki,
        w_in[:, :, OFF_CONV:OFF_CONV + C_CONV], w_in[:, :, OFF_CONV + C_CONV:OFF_POOL], w_in[:, :, OFF_POOL:D_IN],
    ], axis=-1).astype(BF16)
    depth = pool_w.shape[0]
    wt = jnp.swapaxes(jnp.concatenate([
        w_in[:, :, OFF_K:OFF_QI], ki, w_in[:, :, OFF_WI:OFF_CONV],
        jnp.zeros((depth, w_in.shape[1], R_ALL - R_WI - IDX_HEADS), w_in.dtype),
    ], axis=-1), 1, 2).astype(BF16)
    pbd = jnp.zeros((depth, C_POOL, C_POOL), F32)
    for gi in range(len(POOL_WINDOWS)):
        sl = slice(gi * POOL_GROUP, (gi + 1) * POOL_GROUP)
        pbd = pbd.at[:, sl, sl].set(pool_w[:, gi])
    return (wcat, wt, w_out.astype(BF16), conv_pw_w.astype(BF16), pbd.astype(BF16),
            ffn_wg.astype(BF16), ffn_wu.astype(BF16), ffn_wd.astype(BF16))


def kernel(x_prompt, x_sample, cache_k, cache_v, cache_ki, state_conv, state_pool, page_table, c_prompt, c_sample, ada_w, ada_b, norm_g, ffn_wg, ffn_wu, ffn_wd, w_in, w_out, conv_dw_w, conv_dw_b, conv_ln_g, conv_ln_b, conv_pw_w, pool_w, pool_scale, final_g):
    batch, seq, d = x_prompt.shape
    nseq = x_sample.shape[0]
    depth = w_in.shape[0]
    n_pool = cache_k.shape[1]
    n_pages = page_table.shape[1]
    past = n_pages * PAGE_SIZE
    assert x_sample.shape[1] == 1 and d == D_MODEL
    assert seq % TOKEN_TILE == 0 and TOKEN_TILE == KEY_TILE and nseq % 8 == 0

    wcat, wt, wo, pww, pbd, wg, wu, wd = _prep_weights(w_in, w_out, conv_pw_w, pool_w, ffn_wg, ffn_wu, ffn_wd)
    ck = jnp.transpose(cache_k, (0, 1, 3, 4, 2)).reshape(depth, n_pool, D_ATTN, PAGE_SIZE)
    cv_ = jnp.transpose(cache_v, (0, 1, 3, 4, 2)).reshape(depth, n_pool, D_ATTN, PAGE_SIZE)
    cki = jnp.transpose(cache_ki, (0, 1, 3, 2))
    dwb = conv_dw_b.reshape(depth, 1, C_CONV)
    lng = conv_ln_g.reshape(depth, 1, C_CONV)
    lnb = conv_ln_b.reshape(depth, 1, C_CONV)
    psc = pool_scale.reshape(depth, 1, C_POOL)

    rows = batch + nseq
    pad = (-rows) % 8
    c_all = jnp.concatenate([c_prompt, c_sample, jnp.zeros((pad, d), F32)], axis=0)
    mods = _ada(c_all, ada_w, ada_b)

    xp = x_prompt.reshape(batch * seq, d)
    xs = x_sample.reshape(nseq, d)
    outs = [[] for _ in range(10)]
    for l in range(depth):
        mp = [mods[l, :batch, i * d:(i + 1) * d].reshape(batch, 1, d) for i in range(N_MOD)]
        ms = [mods[l, batch:rows, i * d:(i + 1) * d].reshape(1, nseq, d) for i in range(N_MOD)]
        g = [norm_g[l, i].reshape(1, d) for i in range(3)]
        mixw = (conv_dw_w, dwb, lng, lnb, pww, pbd, psc, wo, l)

        xp = _ffn(xp, g[0], mp[0], mp[1], mp[2], wg, wu, wd, l, 0)
        xs = _ffn(xs, g[0], ms[0], ms[1], ms[2], wg, wu, wd, l, 0)

        q, kb, ktf, vtf, vt, qi, kitf, ki2, wit, u, p = _inproj(xp, g[1], mp[3], mp[4], wcat, wt, l)
        a = _dsa_prompt(q, qi, wit, kb, ki2, vt, batch)
        xp = _mix(a, u, p, xp, mp[5], *mixw, batch)
        u3 = u.reshape(batch, seq, C_CONV)
        p3 = p.reshape(batch, seq, C_POOL)
        for lst, val in zip(outs[:5], (ktf, vtf, kitf, u3[:, seq - (CONV_WIDTH - 1):], p3[:, seq - POOL_HIST:])):
            lst.append(val)

        q, kb, ktf, vtf, vt, qi, kitf, ki2, wit, u, p = _inproj(xs, g[1], ms[3], ms[4], wcat, wt, l)
        scores = _sample_scores(page_table, qi, wit, kitf[0].T, cki, l)
        bias = _sample_select(scores.reshape(nseq, -1), past + 1)
        a = _sample_attn(page_table, q, ktf[0].T, vtf[0].T, bias, ck, cv_, l).reshape(nseq, D_ATTN)
        xs = _sample_mix(a, u, state_conv, p, state_pool, xs, ms[5], *mixw, past)
        for lst, val in zip(outs[5:], (ktf, vtf, kitf,
                                       jnp.concatenate([state_conv[l][:, 1:], u[:, None, :]], axis=1),
                                       jnp.concatenate([state_pool[l][:, 1:], p[:, None, :]], axis=1))):
            lst.append(val)

        xp = _ffn(xp, g[2], mp[6], mp[7], mp[8], wg, wu, wd, l, 1)
        xs = _ffn(xs, g[2], ms[6], ms[7], ms[8], wg, wu, wd, l, 1)

    fg = final_g.reshape(1, d)
    y_prompt = _final_norm(xp, fg).reshape(batch, seq, d)
    y_sample = _final_norm(xs, fg).reshape(nseq, 1, d)
    st = [jnp.stack(o) for o in outs]

    def heads_last(kt):
        return jnp.transpose(kt.reshape(depth, kt.shape[1], N_HEADS, HEAD_DIM, kt.shape[3]), (0, 1, 4, 2, 3))

    def per_seq(a):
        return jnp.swapaxes(a, 1, 2)

    return (y_prompt, y_sample,
            heads_last(st[0]), heads_last(st[1]), jnp.swapaxes(st[2], 2, 3), st[3], st[4],
            per_seq(heads_last(st[5])), per_seq(heads_last(st[6])), per_seq(jnp.swapaxes(st[7], 2, 3)), st[8], st[9])
```

```python
import functools
import math

import jax
import jax.numpy as jnp
from jax import lax
from jax.experimental import pallas as pl
from jax.experimental.pallas import tpu as pltpu

F32 = jnp.float32
BF16 = jnp.bfloat16
I32 = jnp.int32

D_MODEL = 1024
N_HEADS = 8
HEAD_DIM = 64
D_ATTN = N_HEADS * HEAD_DIM
IDX_HEADS = 8
IDX_DIM = 64
TOPK_MAX = 256
PAGE_SIZE = 128
C_CONV = 256
CONV_WIDTH = 31
C_POOL = 256
POOL_WINDOWS = (2, 4, 8, 16)
POOL_GROUP = C_POOL // len(POOL_WINDOWS)
POOL_HIST = max(POOL_WINDOWS) - 1
D_FF = 2816
EPS = 1e-6
N_MOD = 9
OFF_Q = 0
OFF_K = OFF_Q + D_ATTN
OFF_V = OFF_K + D_ATTN
OFF_QI = OFF_V + D_ATTN
OFF_KI = OFF_QI + IDX_HEADS * IDX_DIM
OFF_WI = OFF_KI + IDX_DIM
OFF_CONV = OFF_WI + IDX_HEADS
OFF_POOL = OFF_CONV + 2 * C_CONV
D_IN = OFF_POOL + C_POOL

LANES = 128
Q_TILE = 128
KEY_TILE = 512
TOKEN_TILE = 512
FF_TILE = 1408
HALO = 32
MASKED = -1e30
KEY_NEG_INF = -2139095041
KEY_LOWEST = -2139095040
ROW_FOLD = 64
ONES_ROWS = 16
LOG2E = 1.4426950408889634
SHIFT_SLACK = 64.0
VMEM_LIMIT = 48 * 1024 * 1024

C_Q, C_K, C_QI, C_KI2, C_CA, C_CG, C_P = 0, 512, 1024, 1536, 1664, 1920, 2176
D_CAT = 2432
R_K, R_V, R_KI, R_WI = 0, 512, 1024, 1088
R_ALL = 1104


def _sigmoid(x):
    return 1.0 / (1.0 + jnp.exp(-x))


def _dot(a, b):
    return jnp.dot(a, b, preferred_element_type=F32)


def _dot_nt(a, b):
    return lax.dot_general(a, b, (((1,), (1,)), ((), ())), preferred_element_type=F32)


def _params(sem, vmem=VMEM_LIMIT):
    return pltpu.CompilerParams(dimension_semantics=sem, vmem_limit_bytes=vmem)


def _ada_kernel(c_ref, w_ref, b_ref, o_ref):
    c = c_ref[...]
    a = (c * _sigmoid(c)).astype(BF16)
    o_ref[...] = _dot(a, w_ref[...].astype(BF16)) + b_ref[...]


def _ada(c_all, ada_w, ada_b):
    depth, d, n = ada_w.shape
    r = c_all.shape[0]
    tn = 1024
    return pl.pallas_call(
        _ada_kernel,
        out_shape=jax.ShapeDtypeStruct((depth, r, n), F32),
        grid=(depth, n // tn),
        in_specs=[
            pl.BlockSpec((r, d), lambda l, j: (0, 0)),
            pl.BlockSpec((None, d, tn), lambda l, j: (l, 0, j)),
            pl.BlockSpec((None, 1, tn), lambda l, j: (l, 0, j)),
        ],
        out_specs=pl.BlockSpec((None, r, tn), lambda l, j: (l, 0, j)),
        compiler_params=_params(("parallel", "parallel")),
        name="ada",
    )(c_all, ada_w, ada_b.reshape(depth, 1, n))


def _norm_mod(x, g, shift, scale):
    y = x * lax.rsqrt(jnp.mean(x * x, axis=-1, keepdims=True) + EPS) * g
    return y * (1.0 + scale) + shift


def _ffn_kernel(x_ref, g_ref, sh_ref, sc_ref, gt_ref, wg_ref, wu_ref, wd_ref, o_ref, h_ref, acc_ref):
    j = pl.program_id(1)

    @pl.when(j == 0)
    def _():
        h_ref[...] = _norm_mod(x_ref[...], g_ref[...], sh_ref[...], sc_ref[...]).astype(BF16)

    h = h_ref[...]
    g = _dot(h, wg_ref[...])
    u = _dot(h, wu_ref[...])
    part = _dot(((g * _sigmoid(g)) * u).astype(BF16), wd_ref[...])

    @pl.when(j == 0)
    def _():
        acc_ref[...] = part

    @pl.when(j > 0)
    def _():
        acc_ref[...] += part

    @pl.when(j == pl.num_programs(1) - 1)
    def _():
        o_ref[...] = x_ref[...] + (0.5 * gt_ref[...]) * acc_ref[...]


def _mod_spec(mod, tiles_per_group):
    _, r, d = mod.shape
    return pl.BlockSpec((None, r, d), lambda i, *_: (i // tiles_per_group, 0, 0))


def _ffn(x, g, shift, scale, gate, wg, wu, wd, l, s):
    n, d = x.shape
    tm = min(TOKEN_TILE, n)
    tpg = (n // shift.shape[0]) // tm
    nf = D_FF // FF_TILE
    return pl.pallas_call(
        _ffn_kernel,
        out_shape=jax.ShapeDtypeStruct((n, d), F32),
        grid=(n // tm, nf),
        in_specs=[
            pl.BlockSpec((tm, d), lambda i, j: (i, 0)),
            pl.BlockSpec((1, d), lambda i, j: (0, 0)),
            _mod_spec(shift, tpg), _mod_spec(scale, tpg), _mod_spec(gate, tpg),
            pl.BlockSpec((None, None, d, FF_TILE), lambda i, j: (l, s, 0, j)),
            pl.BlockSpec((None, None, d, FF_TILE), lambda i, j: (l, s, 0, j)),
            pl.BlockSpec((None, None, FF_TILE, d), lambda i, j: (l, s, j, 0)),
        ],
        out_specs=pl.BlockSpec((tm, d), lambda i, j: (i, 0)),
        scratch_shapes=[pltpu.VMEM((tm, d), BF16), pltpu.VMEM((tm, d), F32)],
        compiler_params=_params(("parallel", "arbitrary")),
        name="ffn",
    )(x, g, shift, scale, gate, wg, wu, wd)


def _inproj_kernel(x_ref, g_ref, sh_ref, sc_ref, wcat_ref, wt_ref,
                   q_ref, kb_ref, ktf_ref, vtf_ref, vt_ref, qi_ref, kitf_ref, ki2_ref, wit_ref, u_ref, p_ref):
    h = _norm_mod(x_ref[...], g_ref[...], sh_ref[...], sc_ref[...]).astype(BF16)
    z = _dot(h, wcat_ref[...])
    zt = _dot_nt(wt_ref[...], h)
    q_ref[...] = (z[:, C_Q:C_K] * (HEAD_DIM ** -0.5 * LOG2E)).astype(BF16)
    kb_ref[...] = z[:, C_K:C_QI].astype(BF16)
    ktf_ref[...] = zt[R_K:R_V]
    vt = zt[R_V:R_KI]
    vtf_ref[...] = vt
    vt_ref[...] = vt.astype(BF16)
    qi_ref[...] = z[:, C_QI:C_KI2].astype(BF16)
    ki2_ref[...] = z[:, C_KI2:C_CA].astype(BF16)
    kitf_ref[...] = zt[R_KI:R_WI]
    wit_ref[...] = zt[R_WI:R_WI + IDX_HEADS] * IDX_HEADS ** -0.5 * IDX_DIM ** -0.5
    u_ref[...] = z[:, C_CA:C_CG] * _sigmoid(z[:, C_CG:C_P])
    p_ref[...] = z[:, C_P:D_CAT]


def _inproj(x, g, shift, scale, wcat, wt, l):
    n, d = x.shape
    tm = min(TOKEN_TILE, n)
    groups = shift.shape[0]
    tg = n // groups
    tpg = tg // tm
    nt = n // tm
    row = lambda w: pl.BlockSpec((tm, w), lambda i: (i, 0))
    col = lambda w: pl.BlockSpec((None, w, tm), lambda i: (i // tpg, 0, i % tpg))
    out_shape = (
        jax.ShapeDtypeStruct((n, D_ATTN), BF16),
        jax.ShapeDtypeStruct((n, D_ATTN), BF16),
        jax.ShapeDtypeStruct((groups, D_ATTN, tg), F32),
        jax.ShapeDtypeStruct((groups, D_ATTN, tg), F32),
        jax.ShapeDtypeStruct((nt, D_ATTN, tm), BF16),
        jax.ShapeDtypeStruct((n, D_ATTN), BF16),
        jax.ShapeDtypeStruct((groups, IDX_DIM, tg), F32),
        jax.ShapeDtypeStruct((n, 2 * IDX_DIM), BF16),
        jax.ShapeDtypeStruct((IDX_HEADS, n), F32),
        jax.ShapeDtypeStruct((n, C_CONV), F32),
        jax.ShapeDtypeStruct((n, C_POOL), F32),
    )
    out_specs = (
        row(D_ATTN), row(D_ATTN), col(D_ATTN), col(D_ATTN),
        pl.BlockSpec((None, D_ATTN, tm), lambda i: (i, 0, 0)),
        row(D_ATTN), col(IDX_DIM), row(2 * IDX_DIM),
        pl.BlockSpec((IDX_HEADS, tm), lambda i: (0, i)),
        row(C_CONV), row(C_POOL),
    )
    return pl.pallas_call(
        _inproj_kernel,
        out_shape=out_shape,
        grid=(nt,),
        in_specs=[
            pl.BlockSpec((tm, d), lambda i: (i, 0)),
            pl.BlockSpec((1, d), lambda i: (0, 0)),
            _mod_spec(shift, tpg), _mod_spec(scale, tpg),
            pl.BlockSpec((None, d, D_CAT), lambda i: (l, 0, 0)),
            pl.BlockSpec((None, R_ALL, d), lambda i: (l, 0, 0)),
        ],
        out_specs=out_specs,
        compiler_params=_params(("parallel",)),
        name="inproj",
    )(x, g, shift, scale, wcat, wt)


def _key_to_float(key):
    bits = jnp.where(key < 0, key ^ jnp.int32(-2 ** 31), ~key)
    return lax.bitcast_convert_type(bits, F32)


def _kth_largest(count_ge, shape, k):
    def step(i, carry):
        key, n_ge = carry
        cand = key | jnp.left_shift(jnp.int32(1), jnp.int32(31) - i)
        n = count_ge(_key_to_float(cand))
        take = n >= k
        return jnp.where(take, cand, key), jnp.where(take, n, n_ge)

    key, n_ge = lax.fori_loop(0, 32, step, (jnp.zeros(shape, I32), jnp.full(shape, jnp.inf, F32)))
    return _key_to_float(key), n_ge


def _tie_cut(count_eq_below, need, shape, nbits):
    def step(i, x):
        cand = x | jnp.left_shift(jnp.int32(1), jnp.int32(nbits - 1) - i)
        return jnp.where(count_eq_below(cand) < need, cand, x)

    return lax.fori_loop(0, nbits, step, jnp.zeros(shape, I32))


def _ordered_key(x):
    bits = lax.bitcast_convert_type(x, I32)
    return bits ^ (lax.shift_right_arithmetic(bits, 31) & jnp.int32(0x7FFFFFFF))


def _kth_largest_key(count_ge, k, shape):
    sign = jnp.int32(-2 ** 31)

    def step(i, carry):
        prefix, n_ge = carry
        cand = prefix | jnp.left_shift(jnp.int32(1), jnp.int32(31) - i)
        n = count_ge(cand ^ sign)
        take = n >= k
        return jnp.where(take, cand, prefix), jnp.where(take, n, n_ge)

    prefix, n_ge = lax.fori_loop(0, 32, step, (jnp.zeros(shape, I32), jnp.full(shape, jnp.inf, F32)))
    return prefix ^ sign, n_ge


def _select_bias(s, idx, thr, cut):
    return jnp.where(s > thr, 0.0, jnp.where(s == thr, jnp.where(idx <= cut, 0.0, MASKED), MASKED))


def _stage_head_pairs(src_ref, dst_ref):
    lo = lax.broadcasted_iota(I32, (Q_TILE, LANES), 1) < HEAD_DIM
    for p in range(N_HEADS // 2):
        t = src_ref[:, p * LANES:(p + 1) * LANES]
        z = jnp.zeros_like(t)
        dst_ref[p, 0:Q_TILE, 0:LANES] = jnp.where(lo, t, z)
        dst_ref[p, Q_TILE:2 * Q_TILE, 0:LANES] = jnp.where(lo, z, t)


def _fold_rows(x, op):
    rows, lanes = x.shape
    return op(x.reshape(rows // ROW_FOLD, ROW_FOLD, lanes), axis=0)


def _bf16_exact(x):
    bits = lax.bitcast_convert_type(x, I32) & jnp.int32(-65536)
    return lax.bitcast_convert_type(bits, F32)


def _max_to_tile(x):
    x = _fold_rows(x, jnp.max)
    parts = [x[r:r + 8] for r in range(0, ROW_FOLD, 8)]
    while len(parts) > 1:
        parts = [jnp.maximum(parts[i], parts[i + 1]) for i in range(0, len(parts), 2)]
    return parts[0]


def _dsa_prompt_kernel(q_ref, qi_ref, wit_ref, k_ref, ki2_ref, vt_ref, o_ref,
                       s_ref, qp_ref, qip_ref, acc_ref, peak_ref, thr_ref, *, topk, idx_bits):
    j = pl.program_id(1)
    tk = KEY_TILE
    nch = ((j + 1) * Q_TILE + tk - 1) // tk
    npair = N_HEADS // 2

    _stage_head_pairs(qi_ref, qip_ref)
    _stage_head_pairs(q_ref, qp_ref)
    wt = wit_ref[...]
    wrow = [jnp.concatenate([wt[2 * p:2 * p + 1], wt[2 * p + 1:2 * p + 2]], axis=1) for p in range(npair)]
    qpos = j * Q_TILE + lax.broadcasted_iota(I32, (1, LANES), 1)
    krow = lax.broadcasted_iota(I32, (tk, LANES), 0)

    def tile(c):
        return pl.ds(pl.multiple_of(c * tk, tk), tk)

    def for_tiles(body):
        def two(i, carry):
            body(2 * i)
            body(2 * i + 1)
            return carry

        lax.fori_loop(0, nch // 2, two, 0)

        @pl.when(nch % 2 == 1)
        def _():
            body(nch - 1)

    def scores(c):
        kic = ki2_ref[tile(c), :]
        acc = None
        dots = [_dot_nt(kic, qip_ref[p]) for p in range(npair)]
        for p in range(npair):
            r = jnp.maximum(dots[p], 0.0) * wrow[p]
            r = r[:, :LANES] + r[:, LANES:]
            acc = r if acc is None else acc + r
        s_ref[tile(c), :] = jnp.where(krow <= qpos - c * tk, _ordered_key(acc), KEY_NEG_INF)

    for_tiles(scores)

    @pl.when(nch % 2 == 1)
    def _():
        s_ref[tile(nch), :] = jnp.full((tk, LANES), KEY_NEG_INF, I32)

    krow2 = lax.broadcasted_iota(I32, (2 * tk, LANES), 0)

    def tile2(c):
        return pl.ds(pl.multiple_of(c * (2 * tk), 2 * tk), 2 * tk)

    def count(pred):
        def body(c, acc):
            return acc + _fold_rows(pred(s_ref[tile2(c), :], c), jnp.sum)
        acc = lax.fori_loop(0, (nch + 1) // 2, body, jnp.zeros((ROW_FOLD, LANES), F32))
        return jnp.sum(acc, axis=0, keepdims=True)

    thr_ref[...] = jnp.full(thr_ref.shape, KEY_LOWEST, I32)

    @pl.when((j + 1) * Q_TILE > topk)
    def _():
        row = (1, LANES)
        thr, n_ge = _kth_largest_key(lambda t: count(lambda s, c: jnp.where(s >= t, 1.0, 0.0)), topk, row)
        thr_ref[...] = jnp.broadcast_to(thr, thr_ref.shape)

        @pl.when(jnp.max(n_ge) > topk)
        def _():
            need = topk - count(lambda s, c: jnp.where(s > thr, 1.0, 0.0))
            cut = _tie_cut(
                lambda x: count(lambda s, c: jnp.where(s == thr, jnp.where(krow2 + c * (2 * tk) < x, 1.0, 0.0), 0.0)),
                need, row, idx_bits)

            def drop(c, carry):
                s = s_ref[tile2(c), :]
                s_ref[tile2(c), :] = jnp.where(s == thr, jnp.where(krow2 + c * (2 * tk) > cut, KEY_NEG_INF, s), s)
                return carry

            lax.fori_loop(0, (nch + 1) // 2, drop, 0)

    thr = thr_ref[0:1, :]

    lane = lax.broadcasted_iota(I32, (LANES, LANES), 1)
    ones = jnp.ones((ONES_ROWS, tk), BF16)

    def ones_columns(rows):
        return jnp.where(lax.broadcasted_iota(I32, (rows, LANES), 1) < N_HEADS, 1.0, 0.0).astype(BF16)

    def set_shifts(shifts):
        rows = [sh[:, half * LANES:(half + 1) * LANES] for sh in shifts for half in range(2)]
        cols = jnp.concatenate(rows + [jnp.zeros((LANES - len(rows), LANES), F32)], axis=0).T
        for i in range(len(rows)):
            qp_ref[i // 2, (i % 2) * Q_TILE:(i % 2 + 1) * Q_TILE, LANES:2 * LANES] = (
                jnp.where(lane == i, -cols, 0.0).astype(BF16))

    def shifted_scores(rows, p, bias):
        k_aug = jnp.concatenate([k_ref[rows, p * LANES:(p + 1) * LANES], ones_columns(bias.shape[0])], axis=1)
        return _dot_nt(k_aug, qp_ref[p]) + bias

    def attend_pass():
        acc_ref[...] = jnp.zeros(acc_ref.shape, F32)
        peak_ref[...] = jnp.full(peak_ref.shape, MASKED, F32)

        def attend(c):
            bias = jnp.where(s_ref[tile(c), :] >= thr, 0.0, MASKED)
            bias = jnp.concatenate([bias, bias], axis=1)
            ss = [shifted_scores(tile(c), p, bias) for p in range(npair)]
            for p in range(npair):
                peak_ref[p] = jnp.maximum(peak_ref[p], _max_to_tile(ss[p]))
                vt1 = jnp.concatenate([vt_ref[c, p * LANES:(p + 1) * LANES, :], ones], axis=0)
                acc_ref[p] += _dot(vt1, jnp.exp2(ss[p]).astype(BF16))

        for_tiles(attend)
        return [jnp.max(peak_ref[p], axis=0, keepdims=True) for p in range(npair)]

    bias0 = jnp.where(s_ref[0:LANES, :] > KEY_NEG_INF, 0.0, MASKED)
    bias0 = jnp.concatenate([bias0, bias0], axis=1)
    shifts = []
    for p in range(npair):
        qp_ref[p, :, LANES:2 * LANES] = jnp.zeros((2 * Q_TILE, LANES), BF16)
        sh = jnp.max(shifted_scores(pl.ds(0, LANES), p, bias0), axis=0, keepdims=True)
        shifts.append(_bf16_exact(sh))
    set_shifts(shifts)
    peaks = attend_pass()

    worst = jnp.abs(peaks[0])
    for p in range(1, npair):
        worst = jnp.maximum(worst, jnp.abs(peaks[p]))

    @pl.when(jnp.max(worst) > SHIFT_SLACK)
    def _():
        set_shifts([_bf16_exact(shifts[p] + peaks[p]) for p in range(npair)])
        attend_pass()

    even = lax.broadcasted_iota(I32, (LANES, LANES), 0) < HEAD_DIM
    for p in range(npair):
        o = acc_ref[p, 0:LANES, :] / acc_ref[p, LANES:LANES + 1, :]
        ot = jnp.where(even, o[:, :LANES], o[:, LANES:])
        o_ref[:, p * LANES:(p + 1) * LANES] = ot.T.astype(o_ref.dtype)


def _dsa_prompt(q, qi, wit, kb, ki2, vt, batch):
    n = q.shape[0]
    t = n // batch
    nq = t // Q_TILE
    ntk = t // KEY_TILE
    topk = min(TOPK_MAX, t // 4)
    assert topk % Q_TILE == 0
    idx_bits = max(1, math.ceil(math.log2(t)))
    once = dict(pipeline_mode=pl.Buffered(1))
    return pl.pallas_call(
        functools.partial(_dsa_prompt_kernel, topk=topk, idx_bits=idx_bits),
        out_shape=jax.ShapeDtypeStruct((n, D_ATTN), BF16),
        grid=(batch, nq),
        in_specs=[
            pl.BlockSpec((Q_TILE, D_ATTN), lambda b, j: (b * nq + j, 0)),
            pl.BlockSpec((Q_TILE, D_ATTN), lambda b, j: (b * nq + j, 0)),
            pl.BlockSpec((IDX_HEADS, Q_TILE), lambda b, j: (0, b * nq + j)),
            pl.BlockSpec((None, t, D_ATTN), lambda b, j: (b, 0, 0), **once),
            pl.BlockSpec((None, t, 2 * IDX_DIM), lambda b, j: (b, 0, 0), **once),
            pl.BlockSpec((None, ntk, D_ATTN, KEY_TILE), lambda b, j: (b, 0, 0, 0), **once),
        ],
        out_specs=pl.BlockSpec((Q_TILE, D_ATTN), lambda b, j: (b * nq + j, 0)),
        scratch_shapes=[
            pltpu.VMEM((t, LANES), I32),
            pltpu.VMEM((N_HEADS // 2, 2 * Q_TILE, 2 * LANES), BF16),
            pltpu.VMEM((N_HEADS // 2, 2 * Q_TILE, LANES), BF16),
            pltpu.VMEM((N_HEADS // 2, LANES + ONES_ROWS, 2 * LANES), F32),
            pltpu.VMEM((N_HEADS // 2, 8, 2 * LANES), F32),
            pltpu.VMEM((8, LANES), I32),
        ],
        compiler_params=_params(("parallel", "arbitrary")),
        name="dsa_prompt",
    )(q, qi, wit, kb.reshape(batch, t, D_ATTN), ki2.reshape(batch, t, 2 * IDX_DIM),
      vt.reshape(batch, ntk, D_ATTN, KEY_TILE))


def _conv_tail(y, dwb_ref, lng_ref, lnb_ref, pww_ref):
    y = y + dwb_ref[...]
    mu = jnp.mean(y, axis=-1, keepdims=True)
    var = jnp.mean(jnp.square(y - mu), axis=-1, keepdims=True)
    y = (y - mu) * lax.rsqrt(var + EPS) * lng_ref[...] + lnb_ref[...]
    return _dot((y * _sigmoid(y)).astype(BF16), pww_ref[...])


def _merge(a, cv, yp, x, gate, wo_ref):
    m = _dot(a, wo_ref[0:D_ATTN, :])
    m = m + _dot(cv.astype(BF16), wo_ref[D_ATTN:D_ATTN + C_CONV, :])
    m = m + _dot(yp.astype(BF16), wo_ref[D_ATTN + C_CONV:, :])
    return x + gate * m


def _pool_window_lanes(shape):
    grp = lax.broadcasted_iota(I32, shape, len(shape) - 1) // POOL_GROUP
    w = jnp.full(shape, POOL_WINDOWS[-1], I32)
    for gi in range(len(POOL_WINDOWS) - 2, -1, -1):
        w = jnp.where(grp == gi, POOL_WINDOWS[gi], w)
    return grp, w


def _mix_kernel(a_ref, u_ref, uh_ref, p_ref, ph_ref, x_ref, gate_ref, dww_ref, dwb_ref, lng_ref, lnb_ref,
                pww_ref, pbd_ref, psc_ref, wo_ref, o_ref, ub_ref, pb_ref, pc_ref, *, tiles_per_seq):
    tm = u_ref.shape[0]
    i = pl.program_id(0)
    ti = i % tiles_per_seq
    first = ti == 0
    pad = 16

    ub_ref[0:HALO, :] = jnp.where(first, 0.0, uh_ref[...])
    ub_ref[HALO:HALO + tm, :] = u_ref[...]
    y = jnp.zeros((tm, C_CONV), F32)
    for tap in range(CONV_WIDTH):
        y = y + dww_ref[tap:tap + 1, :] * ub_ref[pl.ds(HALO - (CONV_WIDTH - 1) + tap, tm), :]
    cv = _conv_tail(y, dwb_ref, lng_ref, lnb_ref, pww_ref)

    n = HALO + tm
    pb_ref[0:pad, :] = jnp.zeros((pad, C_POOL), F32)
    pc_ref[0:pad, :] = jnp.zeros((pad, C_POOL), F32)
    pb_ref[pad:pad + HALO, :] = jnp.where(first, 0.0, ph_ref[...])
    pb_ref[pad + HALO:pad + n, :] = p_ref[...]
    cur = p_ref[...]
    grp, wl = _pool_window_lanes((tm, C_POOL))
    src, dst = pb_ref, pc_ref
    win = jnp.zeros((tm, C_POOL), F32)
    for gi, w in enumerate(POOL_WINDOWS):
        s = w // 2
        dst[pad:pad + n, :] = src[pl.ds(pad, n), :] + src[pl.ds(pad - s, n), :]
        win = jnp.where(grp == gi, dst[pl.ds(pad + HALO, tm), :], win)
        src, dst = dst, src
    pos = ti * tm + lax.broadcasted_iota(I32, (tm, C_POOL), 0)
    cnt = jnp.minimum(pos + 1, wl).astype(F32)
    d = win / cnt - cur
    yp = _dot(d.astype(BF16), pbd_ref[...]) * psc_ref[...]

    o_ref[...] = _merge(a_ref[...], cv, yp, x_ref[...], gate_ref[...], wo_ref)


def _mix(a, u, p, x, gate, dww, dwb, lng, lnb, pww, pbd, psc, wo, l, batch):
    n, d = x.shape
    tm = TOKEN_TILE
    tps = (n // batch) // tm
    hb = tm // HALO
    halo = lambda w: pl.BlockSpec((HALO, w), lambda i: (jnp.maximum(i * hb - 1, 0), 0))
    lay = lambda *s: pl.BlockSpec((None,) + s, lambda i: (l,) + (0,) * len(s))
    return pl.pallas_call(
        functools.partial(_mix_kernel, tiles_per_seq=tps),
        out_shape=jax.ShapeDtypeStruct((n, d), F32),
        grid=(n // tm,),
        in_specs=[
            pl.BlockSpec((tm, D_ATTN), lambda i: (i, 0)),
            pl.BlockSpec((tm, C_CONV), lambda i: (i, 0)), halo(C_CONV),
            pl.BlockSpec((tm, C_POOL), lambda i: (i, 0)), halo(C_POOL),
            pl.BlockSpec((tm, d), lambda i: (i, 0)),
            _mod_spec(gate, tps),
            lay(CONV_WIDTH, C_CONV), lay(1, C_CONV), lay(1, C_CONV), lay(1, C_CONV),
            lay(C_CONV, C_CONV), lay(C_POOL, C_POOL), lay(1, C_POOL), lay(d, d),
        ],
        out_specs=pl.BlockSpec((tm, d), lambda i: (i, 0)),
        scratch_shapes=[
            pltpu.VMEM((HALO + tm, C_CONV), F32),
            pltpu.VMEM((16 + HALO + tm, C_POOL), F32),
            pltpu.VMEM((16 + HALO + tm, C_POOL), F32),
        ],
        compiler_params=_params(("parallel",)),
        name="mix",
    )(a, u, u, p, p, x, gate, dww, dwb, lng, lnb, pww, pbd, psc, wo)


def _sample_scores_kernel(pt_ref, qi_ref, wi_ref, kin_ref, *rest):
    pages, o_ref = rest[:-1], rest[-1]
    qi = qi_ref[...]
    w = wi_ref[...]
    tiles = []
    for pg in pages:
        d = _dot(qi, pg[...].astype(BF16))
        tiles.append(jnp.sum(jnp.maximum(d, 0.0) * w, axis=0, keepdims=True))
    kin = kin_ref[...].astype(BF16).astype(F32)
    dn = jnp.sum(qi.astype(F32) * kin, axis=1, keepdims=True)
    sn = jnp.sum(jnp.maximum(dn, 0.0) * w, axis=0, keepdims=True)
    lane = lax.broadcasted_iota(I32, (1, LANES), 1)
    tiles.append(jnp.where(lane == 0, sn, -jnp.inf))
    o_ref[...] = jnp.concatenate(tiles, axis=1)


def _page_specs(n_pages, width, l):
    return [pl.BlockSpec((None, None, width, PAGE_SIZE), functools.partial(lambda n, pt, jj: (l, pt[n, jj], 0, 0), jj=jj))
            for jj in range(n_pages)]


def _sample_scores(page_table, qi, wit, kif, cache_ki, l):
    n, n_pages = page_table.shape
    width = n_pages * PAGE_SIZE + LANES
    per_seq = lambda *s: pl.BlockSpec((None,) + s, lambda i, pt: (i,) + (0,) * len(s))
    return pl.pallas_call(
        _sample_scores_kernel,
        out_shape=jax.ShapeDtypeStruct((n, 1, width), F32),
        grid_spec=pltpu.PrefetchScalarGridSpec(
            num_scalar_prefetch=1,
            grid=(n,),
            in_specs=[per_seq(IDX_HEADS, IDX_DIM), per_seq(IDX_HEADS, 1), per_seq(1, IDX_DIM)]
            + _page_specs(n_pages, IDX_DIM, l),
            out_specs=per_seq(1, width),
        ),
        compiler_params=_params(("arbitrary",)),
        name="sample_scores",
    )(page_table, qi.reshape(n, IDX_HEADS, IDX_DIM), wit.T.reshape(n, IDX_HEADS, 1),
      kif.reshape(n, 1, IDX_DIM), *([cache_ki] * n_pages))


def _sample_select_kernel(s_ref, o_ref, *, topk, idx_bits):
    s = s_ref[...]
    idx = lax.broadcasted_iota(I32, s.shape, 1)
    shape = (s.shape[0], 1)

    def count(x):
        return jnp.sum(x, axis=1, keepdims=True)

    thr, _ = _kth_largest(lambda t: count(jnp.where(s >= t, 1.0, 0.0)), shape, topk)
    eq = s == thr
    need = topk - count(jnp.where(s > thr, 1.0, 0.0))
    cut = _tie_cut(lambda x: count(jnp.where(eq, jnp.where(idx < x, 1.0, 0.0), 0.0)), need, shape, idx_bits)
    o_ref[...] = _select_bias(s, idx, thr, cut)


def _sample_select(scores, total):
    n, width = scores.shape
    topk = min(TOPK_MAX, total // 4)
    return pl.pallas_call(
        functools.partial(_sample_select_kernel, topk=topk, idx_bits=max(1, math.ceil(math.log2(width)))),
        out_shape=jax.ShapeDtypeStruct((n, width), F32),
        name="sample_select",
    )(scores)


def _sample_attn_kernel(pt_ref, q_ref, kn_ref, vn_ref, b_ref, *rest):
    n_pages = (len(rest) - 1) // 2
    kp, vp, o_ref = rest[:n_pages], rest[n_pages:2 * n_pages], rest[-1]
    past = n_pages * PAGE_SIZE
    own = (lax.broadcasted_iota(I32, (N_HEADS, D_ATTN), 0)
           == lax.broadcasted_iota(I32, (N_HEADS, D_ATTN), 1) // HEAD_DIM)
    qf = jnp.where(own, jnp.broadcast_to(q_ref[...].astype(F32), (N_HEADS, D_ATTN)), 0.0)
    qm = qf.astype(BF16)
    bias = b_ref[...]
    s = jnp.concatenate([_dot(qm, pg[...].astype(BF16)) for pg in kp], axis=1) + bias[:, :past]
    kn = kn_ref[...].astype(BF16).astype(F32)
    sn = jnp.sum(qf * kn, axis=1, keepdims=True) + bias[:, past:past + 1]
    m = jnp.maximum(jnp.max(s, axis=1, keepdims=True), sn)
    e = jnp.exp2(s - m)
    en = jnp.exp2(sn - m)
    den = jnp.sum(e, axis=1, keepdims=True) + en
    o = en * vn_ref[...].astype(BF16).astype(F32)
    eb = e.astype(BF16)
    for jj, pg in enumerate(vp):
        o = o + _dot_nt(eb[:, jj * PAGE_SIZE:(jj + 1) * PAGE_SIZE], pg[...].astype(BF16))
    o = o / den
    o_ref[...] = jnp.sum(jnp.where(own, o, 0.0), axis=0, keepdims=True).astype(o_ref.dtype)


def _sample_attn(page_table, q, kf, vf, bias, cache_k, cache_v, l):
    n, n_pages = page_table.shape
    width = bias.shape[-1]
    per_seq = lambda *s: pl.BlockSpec((None,) + s, lambda i, pt: (i,) + (0,) * len(s))
    return pl.pallas_call(
        _sample_attn_kernel,
        out_shape=jax.ShapeDtypeStruct((n, 1, D_ATTN), BF16),
        grid_spec=pltpu.PrefetchScalarGridSpec(
            num_scalar_prefetch=1,
            grid=(n,),
            in_specs=[per_seq(1, D_ATTN), per_seq(1, D_ATTN), per_seq(1, D_ATTN), per_seq(1, width)]
            + _page_specs(n_pages, D_ATTN, l) + _page_specs(n_pages, D_ATTN, l),
            out_specs=per_seq(1, D_ATTN),
        ),
        compiler_params=_params(("arbitrary",)),
        name="sample_attn",
    )(page_table, q.reshape(n, 1, D_ATTN), kf.reshape(n, 1, D_ATTN), vf.reshape(n, 1, D_ATTN),
      bias.reshape(n, 1, width), *([cache_k] * n_pages), *([cache_v] * n_pages))


def _sample_mix_kernel(a_ref, u_ref, sc_ref, p_ref, sp_ref, x_ref, gate_ref, dww_ref, dwb_ref, lng_ref, lnb_ref,
                       pww_ref, pbd_ref, psc_ref, wo_ref, o_ref, *, pos):
    u = u_ref[...]
    hist = CONV_WIDTH - 1
    y = dww_ref[hist:hist + 1, :] * u
    for tap in range(hist):
        y = y + dww_ref[tap:tap + 1, :] * sc_ref[tap]
    cv = _conv_tail(y, dwb_ref, lng_ref, lnb_ref, pww_ref)

    cur = p_ref[...]
    grp, wl = _pool_window_lanes(cur.shape)
    run = cur
    win = jnp.zeros_like(cur)
    back = 0
    for gi, w in enumerate(POOL_WINDOWS):
        while back < w - 1:
            back += 1
            run = run + sp_ref[POOL_HIST - back]
        win = jnp.where(grp == gi, run, win)
    cnt = jnp.minimum(pos + 1, wl).astype(F32)
    d = win / cnt - cur
    yp = _dot(d.astype(BF16), pbd_ref[...]) * psc_ref[...]

    o_ref[...] = _merge(a_ref[...], cv, yp, x_ref[...], gate_ref[...], wo_ref)


def _sample_mix(a, u, state_conv, p, state_pool, x, gate, dww, dwb, lng, lnb, pww, pbd, psc, wo, l, pos):
    n, d = x.shape
    full = lambda arr: pl.BlockSpec(arr.shape, lambda i: (0,) * arr.ndim)
    lay = lambda *s: pl.BlockSpec((None,) + s, lambda i: (l,) + (0,) * len(s))
    sc = jnp.swapaxes(state_conv, 1, 2)
    sp = jnp.swapaxes(state_pool, 1, 2)
    return pl.pallas_call(
        functools.partial(_sample_mix_kernel, pos=pos),
        out_shape=jax.ShapeDtypeStruct((n, d), F32),
        grid=(1,),
        in_specs=[
            full(a), full(u), lay(CONV_WIDTH - 1, n, C_CONV), full(p), lay(POOL_HIST, n, C_POOL), full(x),
            pl.BlockSpec((None, n, d), lambda i: (0, 0, 0)),
            lay(CONV_WIDTH, C_CONV), lay(1, C_CONV), lay(1, C_CONV), lay(1, C_CONV),
            lay(C_CONV, C_CONV), lay(C_POOL, C_POOL), lay(1, C_POOL), lay(d, d),
        ],
        out_specs=full(x),
        compiler_params=_params(("arbitrary",)),
        name="sample_mix",
    )(a, u, sc, p, sp, x, gate, dww, dwb, lng, lnb, pww, pbd, psc, wo)


def _final_norm_kernel(x_ref, g_ref, o_ref):
    x = x_ref[...]
    o_ref[...] = x * lax.rsqrt(jnp.mean(x * x, axis=-1, keepdims=True) + EPS) * g_ref[...]


def _final_norm(x, g):
    n, d = x.shape
    tm = min(TOKEN_TILE, n)
    return pl.pallas_call(
        _final_norm_kernel,
        out_shape=jax.ShapeDtypeStruct((n, d), F32),
        grid=(n // tm,),
        in_specs=[pl.BlockSpec((tm, d), lambda i: (i, 0)), pl.BlockSpec((1, d), lambda i: (0, 0))],
        out_specs=pl.BlockSpec((tm, d), lambda i: (i, 0)),
        compiler_params=_params(("parallel",)),
        name="final_norm",
    )(x, g)


def _prep_weights(w_in, w_out, conv_pw_w, pool_w, ffn_wg, ffn_wu, ffn_wd):
    ki = w_in[:, :, OFF_KI:OFF_WI]
    wcat = jnp.concatenate([
        w_in[:, :, OFF_Q:OFF_V], w_in[:, :, OFF_QI:OFF_KI], ki, ki,
        w_in[:, :, OFF_CONV:OFF_CONV + C_CONV], w_in[:, :, OFF_CONV + C_CONV:OFF_POOL], w_in[:, :, OFF_POOL:D_IN],
    ], axis=-1).astype(BF16)
    depth = pool_w.shape[0]
    wt = jnp.swapaxes(jnp.concatenate([
        w_in[:, :, OFF_K:OFF_QI], ki, w_in[:, :, OFF_WI:OFF_CONV],
        jnp.zeros((depth, w_in.shape[1], R_ALL - R_WI - IDX_HEADS), w_in.dtype),
    ], axis=-1), 1, 2).astype(BF16)
    pbd = jnp.zeros((depth, C_POOL, C_POOL), F32)
    for gi in range(len(POOL_WINDOWS)):
        sl = slice(gi * POOL_GROUP, (gi + 1) * POOL_GROUP)
        pbd = pbd.at[:, sl, sl].set(pool_w[:, gi])
    return (wcat, wt, w_out.astype(BF16), conv_pw_w.astype(BF16), pbd.astype(BF16),
            ffn_wg.astype(BF16), ffn_wu.astype(BF16), ffn_wd.astype(BF16))


def kernel(x_prompt, x_sample, cache_k, cache_v, cache_ki, state_conv, state_pool, page_table, c_prompt, c_sample, ada_w, ada_b, norm_g, ffn_wg, ffn_wu, ffn_wd, w_in, w_out, conv_dw_w, conv_dw_b, conv_ln_g, conv_ln_b, conv_pw_w, pool_w, pool_scale, final_g):
    batch, seq, d = x_prompt.shape
    nseq = x_sample.shape[0]
    depth = w_in.shape[0]
    n_pool = cache_k.shape[1]
    n_pages = page_table.shape[1]
    past = n_pages * PAGE_SIZE
    assert x_sample.shape[1] == 1 and d == D_MODEL
    assert seq % TOKEN_TILE == 0 and TOKEN_TILE == KEY_TILE and nseq % 8 == 0

    wcat, wt, wo, pww, pbd, wg, wu, wd = _prep_weights(w_in, w_out, conv_pw_w, pool_w, ffn_wg, ffn_wu, ffn_wd)
    ck = jnp.transpose(cache_k, (0, 1, 3, 4, 2)).reshape(depth, n_pool, D_ATTN, PAGE_SIZE)
    cv_ = jnp.transpose(cache_v, (0, 1, 3, 4, 2)).reshape(depth, n_pool, D_ATTN, PAGE_SIZE)
    cki = jnp.transpose(cache_ki, (0, 1, 3, 2))
    dwb = conv_dw_b.reshape(depth, 1, C_CONV)
    lng = conv_ln_g.reshape(depth, 1, C_CONV)
    lnb = conv_ln_b.reshape(depth, 1, C_CONV)
    psc = pool_scale.reshape(depth, 1, C_POOL)

    rows = batch + nseq
    pad = (-rows) % 8
    c_all = jnp.concatenate([c_prompt, c_sample, jnp.zeros((pad, d), F32)], axis=0)
    mods = _ada(c_all, ada_w, ada_b)

    xp = x_prompt.reshape(batch * seq, d)
    xs = x_sample.reshape(nseq, d)
    outs = [[] for _ in range(10)]
    for l in range(depth):
        mp = [mods[l, :batch, i * d:(i + 1) * d].reshape(batch, 1, d) for i in range(N_MOD)]
        ms = [mods[l, batch:rows, i * d:(i + 1) * d].reshape(1, nseq, d) for i in range(N_MOD)]
        g = [norm_g[l, i].reshape(1, d) for i in range(3)]
        mixw = (conv_dw_w, dwb, lng, lnb, pww, pbd, psc, wo, l)

        xp = _ffn(xp, g[0], mp[0], mp[1], mp[2], wg, wu, wd, l, 0)
        xs = _ffn(xs, g[0], ms[0], ms[1], ms[2], wg, wu, wd, l, 0)

        q, kb, ktf, vtf, vt, qi, kitf, ki2, wit, u, p = _inproj(xp, g[1], mp[3], mp[4], wcat, wt, l)
        a = _dsa_prompt(q, qi, wit, kb, ki2, vt, batch)
        xp = _mix(a, u, p, xp, mp[5], *mixw, batch)
        u3 = u.reshape(batch, seq, C_CONV)
        p3 = p.reshape(batch, seq, C_POOL)
        for lst, val in zip(outs[:5], (ktf, vtf, kitf, u3[:, seq - (CONV_WIDTH - 1):], p3[:, seq - POOL_HIST:])):
            lst.append(val)

        q, kb, ktf, vtf, vt, qi, kitf, ki2, wit, u, p = _inproj(xs, g[1], ms[3], ms[4], wcat, wt, l)
        scores = _sample_scores(page_table, qi, wit, kitf[0].T, cki, l)
        bias = _sample_select(scores.reshape(nseq, -1), past + 1)
        a = _sample_attn(page_table, q, ktf[0].T, vtf[0].T, bias, ck, cv_, l).reshape(nseq, D_ATTN)
        xs = _sample_mix(a, u, state_conv, p, state_pool, xs, ms[5], *mixw, past)
        for lst, val in zip(outs[5:], (ktf, vtf, kitf,
                                       jnp.concatenate([state_conv[l][:, 1:], u[:, None, :]], axis=1),
                                       jnp.concatenate([state_pool[l][:, 1:], p[:, None, :]], axis=1))):
            lst.append(val)

        xp = _ffn(xp, g[2], mp[6], mp[7], mp[8], wg, wu, wd, l, 1)
        xs = _ffn(xs, g[2], ms[6], ms[7], ms[8], wg, wu, wd, l, 1)

    fg = final_g.reshape(1, d)
    y_prompt = _final_norm(xp, fg).reshape(batch, seq, d)
    y_sample = _final_norm(xs, fg).reshape(nseq, 1, d)
    st = [jnp.stack(o) for o in outs]

    def heads_last(kt):
        return jnp.transpose(kt.reshape(depth, kt.shape[1], N_HEADS, HEAD_DIM, kt.shape[3]), (0, 1, 4, 2, 3))

    def per_seq(a):
        return jnp.swapaxes(a, 1, 2)

    return (y_prompt, y_sample,
            heads_last(st[0]), heads_last(st[1]), jnp.swapaxes(st[2], 2, 3), st[3], st[4],
            per_seq(heads_last(st[5])), per_seq(heads_last(st[6])), per_seq(jnp.swapaxes(st[7], 2, 3)), st[8], st[9])
```

```python
import functools
import math

import jax
import jax.numpy as jnp
from jax import lax
from jax.experimental import pallas as pl
from jax.experimental.pallas import tpu as pltpu

F32 = jnp.float32
BF16 = jnp.bfloat16
I32 = jnp.int32

D_MODEL = 1024
N_HEADS = 8
HEAD_DIM = 64
D_ATTN = N_HEADS * HEAD_DIM
IDX_HEADS = 8
IDX_DIM = 64
TOPK_MAX = 256
PAGE_SIZE = 128
C_CONV = 256
CONV_WIDTH = 31
C_POOL = 256
POOL_WINDOWS = (2, 4, 8, 16)
POOL_GROUP = C_POOL // len(POOL_WINDOWS)
POOL_HIST = max(POOL_WINDOWS) - 1
D_FF = 2816
EPS = 1e-6
N_MOD = 9
OFF_Q = 0
OFF_K = OFF_Q + D_ATTN
OFF_V = OFF_K + D_ATTN
OFF_QI = OFF_V + D_ATTN
OFF_KI = OFF_QI + IDX_HEADS * IDX_DIM
OFF_WI = OFF_KI + IDX_DIM
OFF_CONV = OFF_WI + IDX_HEADS
OFF_POOL = OFF_CONV + 2 * C_CONV
D_IN = OFF_POOL + C_POOL

LANES = 128
Q_TILE = 128
KEY_TILE = 512
TOKEN_TILE = 512
FF_TILE = 1408
HALO = 32
MASKED = -1e30
KEY_NEG_INF = -2139095041
KEY_LOWEST = -2139095040
ROW_FOLD = 64
PLANE_BLOCK = 256
ONES_ROWS = 16
LOG2E = 1.4426950408889634
SHIFT_SLACK = 64.0
VMEM_LIMIT = 48 * 1024 * 1024

C_Q, C_K, C_QI, C_KI2, C_CA, C_CG, C_P = 0, 512, 1024, 1536, 1664, 1920, 2176
D_CAT = 2432
R_K, R_V, R_KI, R_WI = 0, 512, 1024, 1088
R_ALL = 1104


def _sigmoid(x):
    return 1.0 / (1.0 + jnp.exp(-x))


def _dot(a, b):
    return jnp.dot(a, b, preferred_element_type=F32)


def _dot_nt(a, b):
    return lax.dot_general(a, b, (((1,), (1,)), ((), ())), preferred_element_type=F32)


def _params(sem, vmem=VMEM_LIMIT):
    return pltpu.CompilerParams(dimension_semantics=sem, vmem_limit_bytes=vmem)


def _ada_kernel(c_ref, w_ref, b_ref, o_ref):
    c = c_ref[...]
    a = (c * _sigmoid(c)).astype(BF16)
    o_ref[...] = _dot(a, w_ref[...].astype(BF16)) + b_ref[...]


def _ada(c_all, ada_w, ada_b):
    depth, d, n = ada_w.shape
    r = c_all.shape[0]
    tn = 1024
    return pl.pallas_call(
        _ada_kernel,
        out_shape=jax.ShapeDtypeStruct((depth, r, n), F32),
        grid=(depth, n // tn),
        in_specs=[
            pl.BlockSpec((r, d), lambda l, j: (0, 0)),
            pl.BlockSpec((None, d, tn), lambda l, j: (l, 0, j)),
            pl.BlockSpec((None, 1, tn), lambda l, j: (l, 0, j)),
        ],
        out_specs=pl.BlockSpec((None, r, tn), lambda l, j: (l, 0, j)),
        compiler_params=_params(("parallel", "parallel")),
        name="ada",
    )(c_all, ada_w, ada_b.reshape(depth, 1, n))


def _norm_mod(x, g, shift, scale):
    y = x * lax.rsqrt(jnp.mean(x * x, axis=-1, keepdims=True) + EPS) * g
    return y * (1.0 + scale) + shift


def _ffn_kernel(x_ref, g_ref, sh_ref, sc_ref, gt_ref, wg_ref, wu_ref, wd_ref, o_ref, h_ref, acc_ref):
    j = pl.program_id(1)

    @pl.when(j == 0)
    def _():
        h_ref[...] = _norm_mod(x_ref[...], g_ref[...], sh_ref[...], sc_ref[...]).astype(BF16)

    h = h_ref[...]
    g = _dot(h, wg_ref[...])
    u = _dot(h, wu_ref[...])
    part = _dot(((g * _sigmoid(g)) * u).astype(BF16), wd_ref[...])

    @pl.when(j == 0)
    def _():
        acc_ref[...] = part

    @pl.when(j > 0)
    def _():
        acc_ref[...] += part

    @pl.when(j == pl.num_programs(1) - 1)
    def _():
        o_ref[...] = x_ref[...] + (0.5 * gt_ref[...]) * acc_ref[...]


def _mod_spec(mod, tiles_per_group):
    _, r, d = mod.shape
    return pl.BlockSpec((None, r, d), lambda i, *_: (i // tiles_per_group, 0, 0))


def _ffn(x, g, shift, scale, gate, wg, wu, wd, l, s):
    n, d = x.shape
    tm = min(TOKEN_TILE, n)
    tpg = (n // shift.shape[0]) // tm
    nf = D_FF // FF_TILE
    return pl.pallas_call(
        _ffn_kernel,
        out_shape=jax.ShapeDtypeStruct((n, d), F32),
        grid=(n // tm, nf),
        in_specs=[
            pl.BlockSpec((tm, d), lambda i, j: (i, 0)),
            pl.BlockSpec((1, d), lambda i, j: (0, 0)),
            _mod_spec(shift, tpg), _mod_spec(scale, tpg), _mod_spec(gate, tpg),
            pl.BlockSpec((None, None, d, FF_TILE), lambda i, j: (l, s, 0, j)),
            pl.BlockSpec((None, None, d, FF_TILE), lambda i, j: (l, s, 0, j)),
            pl.BlockSpec((None, None, FF_TILE, d), lambda i, j: (l, s, j, 0)),
        ],
        out_specs=pl.BlockSpec((tm, d), lambda i, j: (i, 0)),
        scratch_shapes=[pltpu.VMEM((tm, d), BF16), pltpu.VMEM((tm, d), F32)],
        compiler_params=_params(("parallel", "arbitrary")),
        name="ffn",
    )(x, g, shift, scale, gate, wg, wu, wd)


def _inproj_kernel(x_ref, g_ref, sh_ref, sc_ref, wcat_ref, wt_ref,
                   q_ref, kb_ref, ktf_ref, vtf_ref, vt_ref, qi_ref, kitf_ref, ki2_ref, wit_ref, u_ref, p_ref):
    h = _norm_mod(x_ref[...], g_ref[...], sh_ref[...], sc_ref[...]).astype(BF16)
    z = _dot(h, wcat_ref[...])
    zt = _dot_nt(wt_ref[...], h)
    q_ref[...] = (z[:, C_Q:C_K] * (HEAD_DIM ** -0.5 * LOG2E)).astype(BF16)
    kb_ref[...] = z[:, C_K:C_QI].astype(BF16)
    ktf_ref[...] = zt[R_K:R_V]
    vt = zt[R_V:R_KI]
    vtf_ref[...] = vt
    vt_ref[...] = vt.astype(BF16)
    qi_ref[...] = z[:, C_QI:C_KI2].astype(BF16)
    ki2_ref[...] = z[:, C_KI2:C_CA].astype(BF16)
    kitf_ref[...] = zt[R_KI:R_WI]
    wit_ref[...] = zt[R_WI:R_WI + IDX_HEADS] * IDX_HEADS ** -0.5 * IDX_DIM ** -0.5
    u_ref[...] = z[:, C_CA:C_CG] * _sigmoid(z[:, C_CG:C_P])
    p_ref[...] = z[:, C_P:D_CAT]


def _inproj(x, g, shift, scale, wcat, wt, l):
    n, d = x.shape
    tm = min(TOKEN_TILE, n)
    groups = shift.shape[0]
    tg = n // groups
    tpg = tg // tm
    nt = n // tm
    row = lambda w: pl.BlockSpec((tm, w), lambda i: (i, 0))
    col = lambda w: pl.BlockSpec((None, w, tm), lambda i: (i // tpg, 0, i % tpg))
    out_shape = (
        jax.ShapeDtypeStruct((n, D_ATTN), BF16),
        jax.ShapeDtypeStruct((n, D_ATTN), BF16),
        jax.ShapeDtypeStruct((groups, D_ATTN, tg), F32),
        jax.ShapeDtypeStruct((groups, D_ATTN, tg), F32),
        jax.ShapeDtypeStruct((nt, D_ATTN, tm), BF16),
        jax.ShapeDtypeStruct((n, D_ATTN), BF16),
        jax.ShapeDtypeStruct((groups, IDX_DIM, tg), F32),
        jax.ShapeDtypeStruct((n, 2 * IDX_DIM), BF16),
        jax.ShapeDtypeStruct((IDX_HEADS, n), F32),
        jax.ShapeDtypeStruct((n, C_CONV), F32),
        jax.ShapeDtypeStruct((n, C_POOL), F32),
    )
    out_specs = (
        row(D_ATTN), row(D_ATTN), col(D_ATTN), col(D_ATTN),
        pl.BlockSpec((None, D_ATTN, tm), lambda i: (i, 0, 0)),
        row(D_ATTN), col(IDX_DIM), row(2 * IDX_DIM),
        pl.BlockSpec((IDX_HEADS, tm), lambda i: (0, i)),
        row(C_CONV), row(C_POOL),
    )
    return pl.pallas_call(
        _inproj_kernel,
        out_shape=out_shape,
        grid=(nt,),
        in_specs=[
            pl.BlockSpec((tm, d), lambda i: (i, 0)),
            pl.BlockSpec((1, d), lambda i: (0, 0)),
            _mod_spec(shift, tpg), _mod_spec(scale, tpg),
            pl.BlockSpec((None, d, D_CAT), lambda i: (l, 0, 0)),
            pl.BlockSpec((None, R_ALL, d), lambda i: (l, 0, 0)),
        ],
        out_specs=out_specs,
        compiler_params=_params(("parallel",)),
        name="inproj",
    )(x, g, shift, scale, wcat, wt)


def _key_to_float(key):
    bits = jnp.where(key < 0, key ^ jnp.int32(-2 ** 31), ~key)
    return lax.bitcast_convert_type(bits, F32)


def _kth_largest(count_ge, shape, k):
    def step(i, carry):
        key, n_ge = carry
        cand = key | jnp.left_shift(jnp.int32(1), jnp.int32(31) - i)
        n = count_ge(_key_to_float(cand))
        take = n >= k
        return jnp.where(take, cand, key), jnp.where(take, n, n_ge)

    key, n_ge = lax.fori_loop(0, 32, step, (jnp.zeros(shape, I32), jnp.full(shape, jnp.inf, F32)))
    return _key_to_float(key), n_ge


def _tie_cut(count_eq_below, need, shape, nbits):
    def step(i, x):
        cand = x | jnp.left_shift(jnp.int32(1), jnp.int32(nbits - 1) - i)
        return jnp.where(count_eq_below(cand) < need, cand, x)

    return lax.fori_loop(0, nbits, step, jnp.zeros(shape, I32))


def _ordered_key(x):
    bits = lax.bitcast_convert_type(x, I32)
    return bits ^ (lax.shift_right_arithmetic(bits, 31) & jnp.int32(0x7FFFFFFF))


def _bit_transpose32(words):
    a = list(words)
    for j, mask in ((16, 0x0000FFFF), (8, 0x00FF00FF), (4, 0x0F0F0F0F), (2, 0x33333333), (1, 0x55555555)):
        for k in range(32):
            if k & j:
                continue
            t = (a[k] ^ lax.shift_right_logical(a[k + j], jnp.int32(j))) & jnp.int32(mask)
            a[k] = a[k] ^ t
            a[k + j] = a[k + j] ^ lax.shift_left(t, jnp.int32(j))
    return a


def _kth_largest_key(count_ge, k, shape):
    sign = jnp.int32(-2 ** 31)

    def step(i, carry):
        prefix, n_ge = carry
        cand = prefix | jnp.left_shift(jnp.int32(1), jnp.int32(31) - i)
        n = count_ge(cand ^ sign)
        take = n >= k
        return jnp.where(take, cand, prefix), jnp.where(take, n, n_ge)

    prefix, n_ge = lax.fori_loop(0, 32, step, (jnp.zeros(shape, I32), jnp.full(shape, jnp.inf, F32)))
    return prefix ^ sign, n_ge


def _select_bias(s, idx, thr, cut):
    return jnp.where(s > thr, 0.0, jnp.where(s == thr, jnp.where(idx <= cut, 0.0, MASKED), MASKED))


def _stage_head_pairs(src_ref, dst_ref):
    lo = lax.broadcasted_iota(I32, (Q_TILE, LANES), 1) < HEAD_DIM
    for p in range(N_HEADS // 2):
        t = src_ref[:, p * LANES:(p + 1) * LANES]
        z = jnp.zeros_like(t)
        dst_ref[p, 0:Q_TILE, 0:LANES] = jnp.where(lo, t, z)
        dst_ref[p, Q_TILE:2 * Q_TILE, 0:LANES] = jnp.where(lo, z, t)


def _fold_rows(x, op):
    rows, lanes = x.shape
    return op(x.reshape(rows // ROW_FOLD, ROW_FOLD, lanes), axis=0)


def _bf16_exact(x):
    bits = lax.bitcast_convert_type(x, I32) & jnp.int32(-65536)
    return lax.bitcast_convert_type(bits, F32)


def _max_to_tile(x):
    x = _fold_rows(x, jnp.max)
    parts = [x[r:r + 8] for r in range(0, ROW_FOLD, 8)]
    while len(parts) > 1:
        parts = [jnp.maximum(parts[i], parts[i + 1]) for i in range(0, len(parts), 2)]
    return parts[0]


def _dsa_prompt_kernel(q_ref, qi_ref, wit_ref, k_ref, ki2_ref, vt_ref, o_ref,
                       s_ref, planes_ref, active_ref, qp_ref, qip_ref, acc_ref, peak_ref, thr_ref, *, topk, idx_bits):
    j = pl.program_id(1)
    tk = KEY_TILE
    nch = ((j + 1) * Q_TILE + tk - 1) // tk
    npair = N_HEADS // 2

    _stage_head_pairs(qi_ref, qip_ref)
    _stage_head_pairs(q_ref, qp_ref)
    wt = wit_ref[...]
    wrow = [jnp.concatenate([wt[2 * p:2 * p + 1], wt[2 * p + 1:2 * p + 2]], axis=1) for p in range(npair)]
    qpos = j * Q_TILE + lax.broadcasted_iota(I32, (1, LANES), 1)
    krow = lax.broadcasted_iota(I32, (tk, LANES), 0)

    def tile(c):
        return pl.ds(pl.multiple_of(c * tk, tk), tk)

    def for_tiles(body):
        def two(i, carry):
            body(2 * i)
            body(2 * i + 1)
            return carry

        lax.fori_loop(0, nch // 2, two, 0)

        @pl.when(nch % 2 == 1)
        def _():
            body(nch - 1)

    def scores(c):
        kic = ki2_ref[tile(c), :]
        acc = None
        dots = [_dot_nt(kic, qip_ref[p]) for p in range(npair)]
        for p in range(npair):
            r = jnp.maximum(dots[p], 0.0) * wrow[p]
            r = r[:, :LANES] + r[:, LANES:]
            acc = r if acc is None else acc + r
        s_ref[tile(c), :] = jnp.where(krow <= qpos - c * tk, _ordered_key(acc), KEY_NEG_INF)

    for_tiles(scores)

    @pl.when(nch % 2 == 1)
    def _():
        s_ref[tile(nch), :] = jnp.full((tk, LANES), KEY_NEG_INF, I32)

    krow2 = lax.broadcasted_iota(I32, (2 * tk, LANES), 0)

    def tile2(c):
        return pl.ds(pl.multiple_of(c * (2 * tk), 2 * tk), 2 * tk)

    def count(pred):
        def body(c, acc):
            return acc + _fold_rows(pred(s_ref[tile2(c), :], c), jnp.sum)
        acc = lax.fori_loop(0, (nch + 1) // 2, body, jnp.zeros((ROW_FOLD, LANES), F32))
        return jnp.sum(acc, axis=0, keepdims=True)

    thr_ref[...] = jnp.full(thr_ref.shape, KEY_LOWEST, I32)

    @pl.when(j == 0)
    def _():
        planes_ref[...] = jnp.zeros(planes_ref.shape, I32)

    @pl.when((j + 1) * Q_TILE > topk)
    def _():
        row = (1, LANES)
        sign = jnp.int32(-2 ** 31)

        def build_planes(blk, carry):
            base = pl.multiple_of(blk * PLANE_BLOCK, PLANE_BLOCK)
            words = [s_ref[pl.ds(base + 8 * i, 8), :] ^ sign for i in range(32)]
            out = _bit_transpose32(words)
            group = pl.ds(pl.multiple_of(blk * 8, 8), 8)
            for r in range(32):
                planes_ref[r, group, :] = out[r]
            return carry

        lax.fori_loop(0, nch * (tk // PLANE_BLOCK), build_planes, 0)

        ngroups = active_ref.shape[0]
        grow = lax.broadcasted_iota(I32, (ngroups, LANES), 0)
        active_ref[...] = jnp.where(grow < nch * (tk // 32), jnp.int32(-1), jnp.int32(0))

        def popcount_rows(words):
            return jnp.sum(lax.population_count(words).astype(F32), axis=0, keepdims=True)

        def step(i, carry):
            key, need = carry
            active = active_ref[...]
            ones = active & planes_ref[i]
            n_ones = popcount_rows(ones)
            take = n_ones >= need
            active_ref[...] = jnp.where(take, ones, active ^ ones)
            bit = jnp.left_shift(jnp.int32(1), jnp.int32(31) - i)
            return jnp.where(take, key | bit, key), jnp.where(take, need, need - n_ones)

        key, need = lax.fori_loop(0, 32, step, (jnp.zeros(row, I32), jnp.full(row, float(topk), F32)))
        thr = key ^ sign
        n_eq = popcount_rows(active_ref[...])
        thr_ref[...] = jnp.broadcast_to(thr, thr_ref.shape)

        @pl.when(jnp.max(n_eq - need) > 0.0)
        def _():
            cut = _tie_cut(
                lambda x: count(lambda s, c: jnp.where(s == thr, jnp.where(krow2 + c * (2 * tk) < x, 1.0, 0.0), 0.0)),
                need, row, idx_bits)

            def drop(c, carry):
                s = s_ref[tile2(c), :]
                s_ref[tile2(c), :] = jnp.where(s == thr, jnp.where(krow2 + c * (2 * tk) > cut, KEY_NEG_INF, s), s)
                return carry

            lax.fori_loop(0, (nch + 1) // 2, drop, 0)

    thr = thr_ref[0:1, :]

    lane = lax.broadcasted_iota(I32, (LANES, LANES), 1)
    ones = jnp.ones((ONES_ROWS, tk), BF16)

    def ones_columns(rows):
        return jnp.where(lax.broadcasted_iota(I32, (rows, LANES), 1) < N_HEADS, 1.0, 0.0).astype(BF16)

    def set_shifts(shifts):
        rows = [sh[:, half * LANES:(half + 1) * LANES] for sh in shifts for half in range(2)]
        cols = jnp.concatenate(rows + [jnp.zeros((LANES - len(rows), LANES), F32)], axis=0).T
        for i in range(len(rows)):
            qp_ref[i // 2, (i % 2) * Q_TILE:(i % 2 + 1) * Q_TILE, LANES:2 * LANES] = (
                jnp.where(lane == i, -cols, 0.0).astype(BF16))

    def shifted_scores(rows, p, bias):
        k_aug = jnp.concatenate([k_ref[rows, p * LANES:(p + 1) * LANES], ones_columns(bias.shape[0])], axis=1)
        return _dot_nt(k_aug, qp_ref[p]) + bias

    def attend_pass():
        acc_ref[...] = jnp.zeros(acc_ref.shape, F32)
        peak_ref[...] = jnp.full(peak_ref.shape, MASKED, F32)

        def attend(c):
            bias = jnp.where(s_ref[tile(c), :] >= thr, 0.0, MASKED)
            bias = jnp.concatenate([bias, bias], axis=1)
            ss = [shifted_scores(tile(c), p, bias) for p in range(npair)]
            for p in range(npair):
                peak_ref[p] = jnp.maximum(peak_ref[p], _max_to_tile(ss[p]))
                vt1 = jnp.concatenate([vt_ref[c, p * LANES:(p + 1) * LANES, :], ones], axis=0)
                acc_ref[p] += _dot(vt1, jnp.exp2(ss[p]).astype(BF16))

        for_tiles(attend)
        return [jnp.max(peak_ref[p], axis=0, keepdims=True) for p in range(npair)]

    bias0 = jnp.where(s_ref[0:LANES, :] > KEY_NEG_INF, 0.0, MASKED)
    bias0 = jnp.concatenate([bias0, bias0], axis=1)
    shifts = []
    for p in range(npair):
        qp_ref[p, :, LANES:2 * LANES] = jnp.zeros((2 * Q_TILE, LANES), BF16)
        sh = jnp.max(shifted_scores(pl.ds(0, LANES), p, bias0), axis=0, keepdims=True)
        shifts.append(_bf16_exact(sh))
    set_shifts(shifts)
    peaks = attend_pass()

    worst = jnp.abs(peaks[0])
    for p in range(1, npair):
        worst = jnp.maximum(worst, jnp.abs(peaks[p]))

    @pl.when(jnp.max(worst) > SHIFT_SLACK)
    def _():
        set_shifts([_bf16_exact(shifts[p] + peaks[p]) for p in range(npair)])
        attend_pass()

    even = lax.broadcasted_iota(I32, (LANES, LANES), 0) < HEAD_DIM
    for p in range(npair):
        o = acc_ref[p, 0:LANES, :] / acc_ref[p, LANES:LANES + 1, :]
        ot = jnp.where(even, o[:, :LANES], o[:, LANES:])
        o_ref[:, p * LANES:(p + 1) * LANES] = ot.T.astype(o_ref.dtype)


def _dsa_prompt(q, qi, wit, kb, ki2, vt, batch):
    n = q.shape[0]
    t = n // batch
    nq = t // Q_TILE
    ntk = t // KEY_TILE
    topk = min(TOPK_MAX, t // 4)
    assert topk % Q_TILE == 0
    idx_bits = max(1, math.ceil(math.log2(t)))
    once = dict(pipeline_mode=pl.Buffered(1))
    return pl.pallas_call(
        functools.partial(_dsa_prompt_kernel, topk=topk, idx_bits=idx_bits),
        out_shape=jax.ShapeDtypeStruct((n, D_ATTN), BF16),
        grid=(batch, nq),
        in_specs=[
            pl.BlockSpec((Q_TILE, D_ATTN), lambda b, j: (b * nq + j, 0)),
            pl.BlockSpec((Q_TILE, D_ATTN), lambda b, j: (b * nq + j, 0)),
            pl.BlockSpec((IDX_HEADS, Q_TILE), lambda b, j: (0, b * nq + j)),
            pl.BlockSpec((None, t, D_ATTN), lambda b, j: (b, 0, 0), **once),
            pl.BlockSpec((None, t, 2 * IDX_DIM), lambda b, j: (b, 0, 0), **once),
            pl.BlockSpec((None, ntk, D_ATTN, KEY_TILE), lambda b, j: (b, 0, 0, 0), **once),
        ],
        out_specs=pl.BlockSpec((Q_TILE, D_ATTN), lambda b, j: (b * nq + j, 0)),
        scratch_shapes=[
            pltpu.VMEM((t, LANES), I32),
            pltpu.VMEM((32, t // 32, LANES), I32),
            pltpu.VMEM((t // 32, LANES), I32),
            pltpu.VMEM((N_HEADS // 2, 2 * Q_TILE, 2 * LANES), BF16),
            pltpu.VMEM((N_HEADS // 2, 2 * Q_TILE, LANES), BF16),
            pltpu.VMEM((N_HEADS // 2, LANES + ONES_ROWS, 2 * LANES), F32),
            pltpu.VMEM((N_HEADS // 2, 8, 2 * LANES), F32),
            pltpu.VMEM((8, LANES), I32),
        ],
        compiler_params=_params(("parallel", "arbitrary")),
        name="dsa_prompt",
    )(q, qi, wit, kb.reshape(batch, t, D_ATTN), ki2.reshape(batch, t, 2 * IDX_DIM),
      vt.reshape(batch, ntk, D_ATTN, KEY_TILE))


def _conv_tail(y, dwb_ref, lng_ref, lnb_ref, pww_ref):
    y = y + dwb_ref[...]
    mu = jnp.mean(y, axis=-1, keepdims=True)
    var = jnp.mean(jnp.square(y - mu), axis=-1, keepdims=True)
    y = (y - mu) * lax.rsqrt(var + EPS) * lng_ref[...] + lnb_ref[...]
    return _dot((y * _sigmoid(y)).astype(BF16), pww_ref[...])


def _merge(a, cv, yp, x, gate, wo_ref):
    m = _dot(a, wo_ref[0:D_ATTN, :])
    m = m + _dot(cv.astype(BF16), wo_ref[D_ATTN:D_ATTN + C_CONV, :])
    m = m + _dot(yp.astype(BF16), wo_ref[D_ATTN + C_CONV:, :])
    return x + gate * m


def _pool_window_lanes(shape):
    grp = lax.broadcasted_iota(I32, shape, len(shape) - 1) // POOL_GROUP
    w = jnp.full(shape, POOL_WINDOWS[-1], I32)
    for gi in range(len(POOL_WINDOWS) - 2, -1, -1):
        w = jnp.where(grp == gi, POOL_WINDOWS[gi], w)
    return grp, w


def _mix_kernel(a_ref, u_ref, uh_ref, p_ref, ph_ref, x_ref, gate_ref, dww_ref, dwb_ref, lng_ref, lnb_ref,
                pww_ref, pbd_ref, psc_ref, wo_ref, o_ref, ub_ref, pb_ref, pc_ref, *, tiles_per_seq):
    tm = u_ref.shape[0]
    i = pl.program_id(0)
    ti = i % tiles_per_seq
    first = ti == 0
    pad = 16

    ub_ref[0:HALO, :] = jnp.where(first, 0.0, uh_ref[...])
    ub_ref[HALO:HALO + tm, :] = u_ref[...]
    y = jnp.zeros((tm, C_CONV), F32)
    for tap in range(CONV_WIDTH):
        y = y + dww_ref[tap:tap + 1, :] * ub_ref[pl.ds(HALO - (CONV_WIDTH - 1) + tap, tm), :]
    cv = _conv_tail(y, dwb_ref, lng_ref, lnb_ref, pww_ref)

    n = HALO + tm
    pb_ref[0:pad, :] = jnp.zeros((pad, C_POOL), F32)
    pc_ref[0:pad, :] = jnp.zeros((pad, C_POOL), F32)
    pb_ref[pad:pad + HALO, :] = jnp.where(first, 0.0, ph_ref[...])
    pb_ref[pad + HALO:pad + n, :] = p_ref[...]
    cur = p_ref[...]
    grp, wl = _pool_window_lanes((tm, C_POOL))
    src, dst = pb_ref, pc_ref
    win = jnp.zeros((tm, C_POOL), F32)
    for gi, w in enumerate(POOL_WINDOWS):
        s = w // 2
        dst[pad:pad + n, :] = src[pl.ds(pad, n), :] + src[pl.ds(pad - s, n), :]
        win = jnp.where(grp == gi, dst[pl.ds(pad + HALO, tm), :], win)
        src, dst = dst, src
    pos = ti * tm + lax.broadcasted_iota(I32, (tm, C_POOL), 0)
    cnt = jnp.minimum(pos + 1, wl).astype(F32)
    d = win / cnt - cur
    yp = _dot(d.astype(BF16), pbd_ref[...]) * psc_ref[...]

    o_ref[...] = _merge(a_ref[...], cv, yp, x_ref[...], gate_ref[...], wo_ref)


def _mix(a, u, p, x, gate, dww, dwb, lng, lnb, pww, pbd, psc, wo, l, batch):
    n, d = x.shape
    tm = TOKEN_TILE
    tps = (n // batch) // tm
    hb = tm // HALO
    halo = lambda w: pl.BlockSpec((HALO, w), lambda i: (jnp.maximum(i * hb - 1, 0), 0))
    lay = lambda *s: pl.BlockSpec((None,) + s, lambda i: (l,) + (0,) * len(s))
    return pl.pallas_call(
        functools.partial(_mix_kernel, tiles_per_seq=tps),
        out_shape=jax.ShapeDtypeStruct((n, d), F32),
        grid=(n // tm,),
        in_specs=[
            pl.BlockSpec((tm, D_ATTN), lambda i: (i, 0)),
            pl.BlockSpec((tm, C_CONV), lambda i: (i, 0)), halo(C_CONV),
            pl.BlockSpec((tm, C_POOL), lambda i: (i, 0)), halo(C_POOL),
            pl.BlockSpec((tm, d), lambda i: (i, 0)),
            _mod_spec(gate, tps),
            lay(CONV_WIDTH, C_CONV), lay(1, C_CONV), lay(1, C_CONV), lay(1, C_CONV),
            lay(C_CONV, C_CONV), lay(C_POOL, C_POOL), lay(1, C_POOL), lay(d, d),
        ],
        out_specs=pl.BlockSpec((tm, d), lambda i: (i, 0)),
        scratch_shapes=[
            pltpu.VMEM((HALO + tm, C_CONV), F32),
            pltpu.VMEM((16 + HALO + tm, C_POOL), F32),
            pltpu.VMEM((16 + HALO + tm, C_POOL), F32),
        ],
        compiler_params=_params(("parallel",)),
        name="mix",
    )(a, u, u, p, p, x, gate, dww, dwb, lng, lnb, pww, pbd, psc, wo)


def _sample_scores_kernel(pt_ref, qi_ref, wi_ref, kin_ref, *rest):
    pages, o_ref = rest[:-1], rest[-1]
    qi = qi_ref[...]
    w = wi_ref[...]
    tiles = []
    for pg in pages:
        d = _dot(qi, pg[...].astype(BF16))
        tiles.append(jnp.sum(jnp.maximum(d, 0.0) * w, axis=0, keepdims=True))
    kin = kin_ref[...].astype(BF16).astype(F32)
    dn = jnp.sum(qi.astype(F32) * kin, axis=1, keepdims=True)
    sn = jnp.sum(jnp.maximum(dn, 0.0) * w, axis=0, keepdims=True)
    lane = lax.broadcasted_iota(I32, (1, LANES), 1)
    tiles.append(jnp.where(lane == 0, sn, -jnp.inf))
    o_ref[...] = jnp.concatenate(tiles, axis=1)


def _page_specs(n_pages, width, l):
    return [pl.BlockSpec((None, None, width, PAGE_SIZE), functools.partial(lambda n, pt, jj: (l, pt[n, jj], 0, 0), jj=jj))
            for jj in range(n_pages)]


def _sample_scores(page_table, qi, wit, kif, cache_ki, l):
    n, n_pages = page_table.shape
    width = n_pages * PAGE_SIZE + LANES
    per_seq = lambda *s: pl.BlockSpec((None,) + s, lambda i, pt: (i,) + (0,) * len(s))
    return pl.pallas_call(
        _sample_scores_kernel,
        out_shape=jax.ShapeDtypeStruct((n, 1, width), F32),
        grid_spec=pltpu.PrefetchScalarGridSpec(
            num_scalar_prefetch=1,
            grid=(n,),
            in_specs=[per_seq(IDX_HEADS, IDX_DIM), per_seq(IDX_HEADS, 1), per_seq(1, IDX_DIM)]
            + _page_specs(n_pages, IDX_DIM, l),
            out_specs=per_seq(1, width),
        ),
        compiler_params=_params(("arbitrary",)),
        name="sample_scores",
    )(page_table, qi.reshape(n, IDX_HEADS, IDX_DIM), wit.T.reshape(n, IDX_HEADS, 1),
      kif.reshape(n, 1, IDX_DIM), *([cache_ki] * n_pages))


def _sample_select_kernel(s_ref, o_ref, *, topk, idx_bits):
    s = s_ref[...]
    idx = lax.broadcasted_iota(I32, s.shape, 1)
    shape = (s.shape[0], 1)

    def count(x):
        return jnp.sum(x, axis=1, keepdims=True)

    thr, _ = _kth_largest(lambda t: count(jnp.where(s >= t, 1.0, 0.0)), shape, topk)
    eq = s == thr
    need = topk - count(jnp.where(s > thr, 1.0, 0.0))
    cut = _tie_cut(lambda x: count(jnp.where(eq, jnp.where(idx < x, 1.0, 0.0), 0.0)), need, shape, idx_bits)
    o_ref[...] = _select_bias(s, idx, thr, cut)


def _sample_select(scores, total):
    n, width = scores.shape
    topk = min(TOPK_MAX, total // 4)
    return pl.pallas_call(
        functools.partial(_sample_select_kernel, topk=topk, idx_bits=max(1, math.ceil(math.log2(width)))),
        out_shape=jax.ShapeDtypeStruct((n, width), F32),
        name="sample_select",
    )(scores)


def _sample_attn_kernel(pt_ref, q_ref, kn_ref, vn_ref, b_ref, *rest):
    n_pages = (len(rest) - 1) // 2
    kp, vp, o_ref = rest[:n_pages], rest[n_pages:2 * n_pages], rest[-1]
    past = n_pages * PAGE_SIZE
    own = (lax.broadcasted_iota(I32, (N_HEADS, D_ATTN), 0)
           == lax.broadcasted_iota(I32, (N_HEADS, D_ATTN), 1) // HEAD_DIM)
    qf = jnp.where(own, jnp.broadcast_to(q_ref[...].astype(F32), (N_HEADS, D_ATTN)), 0.0)
    qm = qf.astype(BF16)
    bias = b_ref[...]
    s = jnp.concatenate([_dot(qm, pg[...].astype(BF16)) for pg in kp], axis=1) + bias[:, :past]
    kn = kn_ref[...].astype(BF16).astype(F32)
    sn = jnp.sum(qf * kn, axis=1, keepdims=True) + bias[:, past:past + 1]
    m = jnp.maximum(jnp.max(s, axis=1, keepdims=True), sn)
    e = jnp.exp2(s - m)
    en = jnp.exp2(sn - m)
    den = jnp.sum(e, axis=1, keepdims=True) + en
    o = en * vn_ref[...].astype(BF16).astype(F32)
    eb = e.astype(BF16)
    for jj, pg in enumerate(vp):
        o = o + _dot_nt(eb[:, jj * PAGE_SIZE:(jj + 1) * PAGE_SIZE], pg[...].astype(BF16))
    o = o / den
    o_ref[...] = jnp.sum(jnp.where(own, o, 0.0), axis=0, keepdims=True).astype(o_ref.dtype)


def _sample_attn(page_table, q, kf, vf, bias, cache_k, cache_v, l):
    n, n_pages = page_table.shape
    width = bias.shape[-1]
    per_seq = lambda *s: pl.BlockSpec((None,) + s, lambda i, pt: (i,) + (0,) * len(s))
    return pl.pallas_call(
        _sample_attn_kernel,
        out_shape=jax.ShapeDtypeStruct((n, 1, D_ATTN), BF16),
        grid_spec=pltpu.PrefetchScalarGridSpec(
            num_scalar_prefetch=1,
            grid=(n,),
            in_specs=[per_seq(1, D_ATTN), per_seq(1, D_ATTN), per_seq(1, D_ATTN), per_seq(1, width)]
            + _page_specs(n_pages, D_ATTN, l) + _page_specs(n_pages, D_ATTN, l),
            out_specs=per_seq(1, D_ATTN),
        ),
        compiler_params=_params(("arbitrary",)),
        name="sample_attn",
    )(page_table, q.reshape(n, 1, D_ATTN), kf.reshape(n, 1, D_ATTN), vf.reshape(n, 1, D_ATTN),
      bias.reshape(n, 1, width), *([cache_k] * n_pages), *([cache_v] * n_pages))


def _sample_mix_kernel(a_ref, u_ref, sc_ref, p_ref, sp_ref, x_ref, gate_ref, dww_ref, dwb_ref, lng_ref, lnb_ref,
                       pww_ref, pbd_ref, psc_ref, wo_ref, o_ref, *, pos):
    u = u_ref[...]
    hist = CONV_WIDTH - 1
    y = dww_ref[hist:hist + 1, :] * u
    for tap in range(hist):
        y = y + dww_ref[tap:tap + 1, :] * sc_ref[tap]
    cv = _conv_tail(y, dwb_ref, lng_ref, lnb_ref, pww_ref)

    cur = p_ref[...]
    grp, wl = _pool_window_lanes(cur.shape)
    run = cur
    win = jnp.zeros_like(cur)
    back = 0
    for gi, w in enumerate(POOL_WINDOWS):
        while back < w - 1:
            back += 1
            run = run + sp_ref[POOL_HIST - back]
        win = jnp.where(grp == gi, run, win)
    cnt = jnp.minimum(pos + 1, wl).astype(F32)
    d = win / cnt - cur
    yp = _dot(d.astype(BF16), pbd_ref[...]) * psc_ref[...]

    o_ref[...] = _merge(a_ref[...], cv, yp, x_ref[...], gate_ref[...], wo_ref)


def _sample_mix(a, u, state_conv, p, state_pool, x, gate, dww, dwb, lng, lnb, pww, pbd, psc, wo, l, pos):
    n, d = x.shape
    full = lambda arr: pl.BlockSpec(arr.shape, lambda i: (0,) * arr.ndim)
    lay = lambda *s: pl.BlockSpec((None,) + s, lambda i: (l,) + (0,) * len(s))
    sc = jnp.swapaxes(state_conv, 1, 2)
    sp = jnp.swapaxes(state_pool, 1, 2)
    return pl.pallas_call(
        functools.partial(_sample_mix_kernel, pos=pos),
        out_shape=jax.ShapeDtypeStruct((n, d), F32),
        grid=(1,),
        in_specs=[
            full(a), full(u), lay(CONV_WIDTH - 1, n, C_CONV), full(p), lay(POOL_HIST, n, C_POOL), full(x),
            pl.BlockSpec((None, n, d), lambda i: (0, 0, 0)),
            lay(CONV_WIDTH, C_CONV), lay(1, C_CONV), lay(1, C_CONV), lay(1, C_CONV),
            lay(C_CONV, C_CONV), lay(C_POOL, C_POOL), lay(1, C_POOL), lay(d, d),
        ],
        out_specs=full(x),
        compiler_params=_params(("arbitrary",)),
        name="sample_mix",
    )(a, u, sc, p, sp, x, gate, dww, dwb, lng, lnb, pww, pbd, psc, wo)


def _final_norm_kernel(x_ref, g_ref, o_ref):
    x = x_ref[...]
    o_ref[...] = x * lax.rsqrt(jnp.mean(x * x, axis=-1, keepdims=True) + EPS) * g_ref[...]


def _final_norm(x, g):
    n, d = x.shape
    tm = min(TOKEN_TILE, n)
    return pl.pallas_call(
        _final_norm_kernel,
        out_shape=jax.ShapeDtypeStruct((n, d), F32),
        grid=(n // tm,),
        in_specs=[pl.BlockSpec((tm, d), lambda i: (i, 0)), pl.BlockSpec((1, d), lambda i: (0, 0))],
        out_specs=pl.BlockSpec((tm, d), lambda i: (i, 0)),
        compiler_params=_params(("parallel",)),
        name="final_norm",
    )(x, g)


def _prep_weights(w_in, w_out, conv_pw_w, pool_w, ffn_wg, ffn_wu, ffn_wd):
    ki = w_in[:, :, OFF_KI:OFF_WI]
    wcat = jnp.concatenate([
        w_in[:, :, OFF_Q:OFF_V], w_in[:, :, OFF_QI:OFF_KI], ki, ki,
        w_in[:, :, OFF_CONV:OFF_CONV + C_CONV], w_in[:, :, OFF_CONV + C_CONV:OFF_POOL], w_in[:, :, OFF_POOL:D_IN],
    ], axis=-1).astype(BF16)
    depth = pool_w.shape[0]
    wt = jnp.swapaxes(jnp.concatenate([
        w_in[:, :, OFF_K:OFF_QI], ki, w_in[:, :, OFF_WI:OFF_CONV],
        jnp.zeros((depth, w_in.shape[1], R_ALL - R_WI - IDX_HEADS), w_in.dtype),
    ], axis=-1), 1, 2).astype(BF16)
    pbd = jnp.zeros((depth, C_POOL, C_POOL), F32)
    for gi in range(len(POOL_WINDOWS)):
        sl = slice(gi * POOL_GROUP, (gi + 1) * POOL_GROUP)
        pbd = pbd.at[:, sl, sl].set(pool_w[:, gi])
    return (wcat, wt, w_out.astype(BF16), conv_pw_w.astype(BF16), pbd.astype(BF16),
            ffn_wg.astype(BF16), ffn_wu.astype(BF16), ffn_wd.astype(BF16))


def kernel(x_prompt, x_sample, cache_k, cache_v, cache_ki, state_conv, state_pool, page_table, c_prompt, c_sample, ada_w, ada_b, norm_g, ffn_wg, ffn_wu, ffn_wd, w_in, w_out, conv_dw_w, conv_dw_b, conv_ln_g, conv_ln_b, conv_pw_w, pool_w, pool_scale, final_g):
    batch, seq, d = x_prompt.shape
    nseq = x_sample.shape[0]
    depth = w_in.shape[0]
    n_pool = cache_k.shape[1]
    n_pages = page_table.shape[1]
    past = n_pages * PAGE_SIZE
    assert x_sample.shape[1] == 1 and d == D_MODEL
    assert seq % TOKEN_TILE == 0 and TOKEN_TILE == KEY_TILE and nseq % 8 == 0

    wcat, wt, wo, pww, pbd, wg, wu, wd = _prep_weights(w_in, w_out, conv_pw_w, pool_w, ffn_wg, ffn_wu, ffn_wd)
    ck = jnp.transpose(cache_k, (0, 1, 3, 4, 2)).reshape(depth, n_pool, D_ATTN, PAGE_SIZE)
    cv_ = jnp.transpose(cache_v, (0, 1, 3, 4, 2)).reshape(depth, n_pool, D_ATTN, PAGE_SIZE)
    cki = jnp.transpose(cache_ki, (0, 1, 3, 2))
    dwb = conv_dw_b.reshape(depth, 1, C_CONV)
    lng = conv_ln_g.reshape(depth, 1, C_CONV)
    lnb = conv_ln_b.reshape(depth, 1, C_CONV)
    psc = pool_scale.reshape(depth, 1, C_POOL)

    rows = batch + nseq
    pad = (-rows) % 8
    c_all = jnp.concatenate([c_prompt, c_sample, jnp.zeros((pad, d), F32)], axis=0)
    mods = _ada(c_all, ada_w, ada_b)

    xp = x_prompt.reshape(batch * seq, d)
    xs = x_sample.reshape(nseq, d)
    outs = [[] for _ in range(10)]
    for l in range(depth):
        mp = [mods[l, :batch, i * d:(i + 1) * d].reshape(batch, 1, d) for i in range(N_MOD)]
        ms = [mods[l, batch:rows, i * d:(i + 1) * d].reshape(1, nseq, d) for i in range(N_MOD)]
        g = [norm_g[l, i].reshape(1, d) for i in range(3)]
        mixw = (conv_dw_w, dwb, lng, lnb, pww, pbd, psc, wo, l)

        xp = _ffn(xp, g[0], mp[0], mp[1], mp[2], wg, wu, wd, l, 0)
        xs = _ffn(xs, g[0], ms[0], ms[1], ms[2], wg, wu, wd, l, 0)

        q, kb, ktf, vtf, vt, qi, kitf, ki2, wit, u, p = _inproj(xp, g[1], mp[3], mp[4], wcat, wt, l)
        a = _dsa_prompt(q, qi, wit, kb, ki2, vt, batch)
        xp = _mix(a, u, p, xp, mp[5], *mixw, batch)
        u3 = u.reshape(batch, seq, C_CONV)
        p3 = p.reshape(batch, seq, C_POOL)
        for lst, val in zip(outs[:5], (ktf, vtf, kitf, u3[:, seq - (CONV_WIDTH - 1):], p3[:, seq - POOL_HIST:])):
            lst.append(val)

        q, kb, ktf, vtf, vt, qi, kitf, ki2, wit, u, p = _inproj(xs, g[1], ms[3], ms[4], wcat, wt, l)
        scores = _sample_scores(page_table, qi, wit, kitf[0].T, cki, l)
        bias = _sample_select(scores.reshape(nseq, -1), past + 1)
        a = _sample_attn(page_table, q, ktf[0].T, vtf[0].T, bias, ck, cv_, l).reshape(nseq, D_ATTN)
        xs = _sample_mix(a, u, state_conv, p, state_pool, xs, ms[5], *mixw, past)
        for lst, val in zip(outs[5:], (ktf, vtf, kitf,
                                       jnp.concatenate([state_conv[l][:, 1:], u[:, None, :]], axis=1),
                                       jnp.concatenate([state_pool[l][:, 1:], p[:, None, :]], axis=1))):
            lst.append(val)

        xp = _ffn(xp, g[2], mp[6], mp[7], mp[8], wg, wu, wd, l, 1)
        xs = _ffn(xs, g[2], ms[6], ms[7], ms[8], wg, wu, wd, l, 1)

    fg = final_g.reshape(1, d)
    y_prompt = _final_norm(xp, fg).reshape(batch, seq, d)
    y_sample = _final_norm(xs, fg).reshape(nseq, 1, d)
    st = [jnp.stack(o) for o in outs]

    def heads_last(kt):
        return jnp.transpose(kt.reshape(depth, kt.shape[1], N_HEADS, HEAD_DIM, kt.shape[3]), (0, 1, 4, 2, 3))

    def per_seq(a):
        return jnp.swapaxes(a, 1, 2)

    return (y_prompt, y_sample,
            heads_last(st[0]), heads_last(st[1]), jnp.swapaxes(st[2], 2, 3), st[3], st[4],
            per_seq(heads_last(st[5])), per_seq(heads_last(st[6])), per_seq(jnp.swapaxes(st[7], 2, 3)), st[8], st[9])
```

```python
import functools
import math

import jax
import jax.numpy as jnp
from jax import lax
from jax.experimental import pallas as pl
from jax.experimental.pallas import tpu as pltpu

F32 = jnp.float32
BF16 = jnp.bfloat16
I32 = jnp.int32

D_MODEL = 1024
N_HEADS = 8
HEAD_DIM = 64
D_ATTN = N_HEADS * HEAD_DIM
IDX_HEADS = 8
IDX_DIM = 64
TOPK_MAX = 256
PAGE_SIZE = 128
C_CONV = 256
CONV_WIDTH = 31
C_POOL = 256
POOL_WINDOWS = (2, 4, 8, 16)
POOL_GROUP = C_POOL // len(POOL_WINDOWS)
POOL_HIST = max(POOL_WINDOWS) - 1
D_FF = 2816
EPS = 1e-6
N_MOD = 9
OFF_Q = 0
OFF_K = OFF_Q + D_ATTN
OFF_V = OFF_K + D_ATTN
OFF_QI = OFF_V + D_ATTN
OFF_KI = OFF_QI + IDX_HEADS * IDX_DIM
OFF_WI = OFF_KI + IDX_DIM
OFF_CONV = OFF_WI + IDX_HEADS
OFF_POOL = OFF_CONV + 2 * C_CONV
D_IN = OFF_POOL + C_POOL

LANES = 128
Q_TILE = 128
KEY_TILE = 512
TOKEN_TILE = 512
FF_TILE = 1408
HALO = 32
MASKED = -1e30
KEY_NEG_INF = -2139095041
KEY_LOWEST = -2139095040
ROW_FOLD = 64
PLANE_BLOCK = 256
ONES_ROWS = 16
LOG2E = 1.4426950408889634
SHIFT_SLACK = 64.0
VMEM_LIMIT = 48 * 1024 * 1024

C_Q, C_K, C_QI, C_KI2, C_CA, C_CG, C_P = 0, 512, 1024, 1536, 1664, 1920, 2176
D_CAT = 2432
R_K, R_V, R_KI, R_WI = 0, 512, 1024, 1088
R_ALL = 1104


def _sigmoid(x):
    return 1.0 / (1.0 + jnp.exp(-x))


def _dot(a, b):
    return jnp.dot(a, b, preferred_element_type=F32)


def _dot_nt(a, b):
    return lax.dot_general(a, b, (((1,), (1,)), ((), ())), preferred_element_type=F32)


def _params(sem, vmem=VMEM_LIMIT):
    return pltpu.CompilerParams(dimension_semantics=sem, vmem_limit_bytes=vmem)


def _ada_kernel(c_ref, w_ref, b_ref, o_ref):
    c = c_ref[...]
    a = (c * _sigmoid(c)).astype(BF16)
    o_ref[...] = _dot(a, w_ref[...].astype(BF16)) + b_ref[...]


def _ada(c_all, ada_w, ada_b):
    depth, d, n = ada_w.shape
    r = c_all.shape[0]
    tn = 1024
    return pl.pallas_call(
        _ada_kernel,
        out_shape=jax.ShapeDtypeStruct((depth, r, n), F32),
        grid=(depth, n // tn),
        in_specs=[
            pl.BlockSpec((r, d), lambda l, j: (0, 0)),
            pl.BlockSpec((None, d, tn), lambda l, j: (l, 0, j)),
            pl.BlockSpec((None, 1, tn), lambda l, j: (l, 0, j)),
        ],
        out_specs=pl.BlockSpec((None, r, tn), lambda l, j: (l, 0, j)),
        compiler_params=_params(("parallel", "parallel")),
        name="ada",
    )(c_all, ada_w, ada_b.reshape(depth, 1, n))


def _norm_mod(x, g, shift, scale):
    y = x * lax.rsqrt(jnp.mean(x * x, axis=-1, keepdims=True) + EPS) * g
    return y * (1.0 + scale) + shift


def _ffn_kernel(x_ref, g_ref, sh_ref, sc_ref, gt_ref, wg_ref, wu_ref, wd_ref, o_ref, h_ref, acc_ref):
    j = pl.program_id(1)

    @pl.when(j == 0)
    def _():
        h_ref[...] = _norm_mod(x_ref[...], g_ref[...], sh_ref[...], sc_ref[...]).astype(BF16)

    h = h_ref[...]
    g = _dot(h, wg_ref[...])
    u = _dot(h, wu_ref[...])
    part = _dot(((g * _sigmoid(g)) * u).astype(BF16), wd_ref[...])

    @pl.when(j == 0)
    def _():
        acc_ref[...] = part

    @pl.when(j > 0)
    def _():
        acc_ref[...] += part

    @pl.when(j == pl.num_programs(1) - 1)
    def _():
        o_ref[...] = x_ref[...] + (0.5 * gt_ref[...]) * acc_ref[...]


def _mod_spec(mod, tiles_per_group):
    _, r, d = mod.shape
    return pl.BlockSpec((None, r, d), lambda i, *_: (i // tiles_per_group, 0, 0))


def _ffn(x, g, shift, scale, gate, wg, wu, wd, l, s):
    n, d = x.shape
    tm = min(TOKEN_TILE, n)
    tpg = (n // shift.shape[0]) // tm
    nf = D_FF // FF_TILE
    return pl.pallas_call(
        _ffn_kernel,
        out_shape=jax.ShapeDtypeStruct((n, d), F32),
        grid=(n // tm, nf),
        in_specs=[
            pl.BlockSpec((tm, d), lambda i, j: (i, 0)),
            pl.BlockSpec((1, d), lambda i, j: (0, 0)),
            _mod_spec(shift, tpg), _mod_spec(scale, tpg), _mod_spec(gate, tpg),
            pl.BlockSpec((None, None, d, FF_TILE), lambda i, j: (l, s, 0, j)),
            pl.BlockSpec((None, None, d, FF_TILE), lambda i, j: (l, s, 0, j)),
            pl.BlockSpec((None, None, FF_TILE, d), lambda i, j: (l, s, j, 0)),
        ],
        out_specs=pl.BlockSpec((tm, d), lambda i, j: (i, 0)),
        scratch_shapes=[pltpu.VMEM((tm, d), BF16), pltpu.VMEM((tm, d), F32)],
        compiler_params=_params(("parallel", "arbitrary")),
        name="ffn",
    )(x, g, shift, scale, gate, wg, wu, wd)


def _inproj_kernel(x_ref, g_ref, sh_ref, sc_ref, wcat_ref, wt_ref,
                   q_ref, kb_ref, ktf_ref, vtf_ref, vt_ref, qi_ref, kitf_ref, ki2_ref, wit_ref, u_ref, p_ref):
    h = _norm_mod(x_ref[...], g_ref[...], sh_ref[...], sc_ref[...]).astype(BF16)
    z = _dot(h, wcat_ref[...])
    zt = _dot_nt(wt_ref[...], h)
    q_ref[...] = (z[:, C_Q:C_K] * (HEAD_DIM ** -0.5 * LOG2E)).astype(BF16)
    kb_ref[...] = z[:, C_K:C_QI].astype(BF16)
    ktf_ref[...] = zt[R_K:R_V]
    vt = zt[R_V:R_KI]
    vtf_ref[...] = vt
    vt_ref[...] = vt.astype(BF16)
    qi_ref[...] = z[:, C_QI:C_KI2].astype(BF16)
    ki2_ref[...] = z[:, C_KI2:C_CA].astype(BF16)
    kitf_ref[...] = zt[R_KI:R_WI]
    wit_ref[...] = zt[R_WI:R_WI + IDX_HEADS] * IDX_HEADS ** -0.5 * IDX_DIM ** -0.5
    u_ref[...] = z[:, C_CA:C_CG] * _sigmoid(z[:, C_CG:C_P])
    p_ref[...] = z[:, C_P:D_CAT]


def _inproj(x, g, shift, scale, wcat, wt, l):
    n, d = x.shape
    tm = min(TOKEN_TILE, n)
    groups = shift.shape[0]
    tg = n // groups
    tpg = tg // tm
    nt = n // tm
    row = lambda w: pl.BlockSpec((tm, w), lambda i: (i, 0))
    col = lambda w: pl.BlockSpec((None, w, tm), lambda i: (i // tpg, 0, i % tpg))
    out_shape = (
        jax.ShapeDtypeStruct((n, D_ATTN), BF16),
        jax.ShapeDtypeStruct((n, D_ATTN), BF16),
        jax.ShapeDtypeStruct((groups, D_ATTN, tg), F32),
        jax.ShapeDtypeStruct((groups, D_ATTN, tg), F32),
        jax.ShapeDtypeStruct((nt, D_ATTN, tm), BF16),
        jax.ShapeDtypeStruct((n, D_ATTN), BF16),
        jax.ShapeDtypeStruct((groups, IDX_DIM, tg), F32),
        jax.ShapeDtypeStruct((n, 2 * IDX_DIM), BF16),
        jax.ShapeDtypeStruct((IDX_HEADS, n), F32),
        jax.ShapeDtypeStruct((n, C_CONV), F32),
        jax.ShapeDtypeStruct((n, C_POOL), F32),
    )
    out_specs = (
        row(D_ATTN), row(D_ATTN), col(D_ATTN), col(D_ATTN),
        pl.BlockSpec((None, D_ATTN, tm), lambda i: (i, 0, 0)),
        row(D_ATTN), col(IDX_DIM), row(2 * IDX_DIM),
        pl.BlockSpec((IDX_HEADS, tm), lambda i: (0, i)),
        row(C_CONV), row(C_POOL),
    )
    return pl.pallas_call(
        _inproj_kernel,
        out_shape=out_shape,
        grid=(nt,),
        in_specs=[
            pl.BlockSpec((tm, d), lambda i: (i, 0)),
            pl.BlockSpec((1, d), lambda i: (0, 0)),
            _mod_spec(shift, tpg), _mod_spec(scale, tpg),
            pl.BlockSpec((None, d, D_CAT), lambda i: (l, 0, 0)),
            pl.BlockSpec((None, R_ALL, d), lambda i: (l, 0, 0)),
        ],
        out_specs=out_specs,
        compiler_params=_params(("parallel",)),
        name="inproj",
    )(x, g, shift, scale, wcat, wt)


def _key_to_float(key):
    bits = jnp.where(key < 0, key ^ jnp.int32(-2 ** 31), ~key)
    return lax.bitcast_convert_type(bits, F32)


def _kth_largest(count_ge, shape, k):
    def step(i, carry):
        key, n_ge = carry
        cand = key | jnp.left_shift(jnp.int32(1), jnp.int32(31) - i)
        n = count_ge(_key_to_float(cand))
        take = n >= k
        return jnp.where(take, cand, key), jnp.where(take, n, n_ge)

    key, n_ge = lax.fori_loop(0, 32, step, (jnp.zeros(shape, I32), jnp.full(shape, jnp.inf, F32)))
    return _key_to_float(key), n_ge


def _tie_cut(count_eq_below, need, shape, nbits):
    def step(i, x):
        cand = x | jnp.left_shift(jnp.int32(1), jnp.int32(nbits - 1) - i)
        return jnp.where(count_eq_below(cand) < need, cand, x)

    return lax.fori_loop(0, nbits, step, jnp.zeros(shape, I32))


def _ordered_key(x):
    bits = lax.bitcast_convert_type(x, I32)
    return bits ^ (lax.shift_right_arithmetic(bits, 31) & jnp.int32(0x7FFFFFFF))


def _bit_transpose32(words):
    a = list(words)
    for j, mask in ((16, 0x0000FFFF), (8, 0x00FF00FF), (4, 0x0F0F0F0F), (2, 0x33333333), (1, 0x55555555)):
        for k in range(32):
            if k & j:
                continue
            t = (a[k] ^ lax.shift_right_logical(a[k + j], jnp.int32(j))) & jnp.int32(mask)
            a[k] = a[k] ^ t
            a[k + j] = a[k + j] ^ lax.shift_left(t, jnp.int32(j))
    return a


def _kth_largest_key(count_ge, k, shape):
    sign = jnp.int32(-2 ** 31)

    def step(i, carry):
        prefix, n_ge = carry
        cand = prefix | jnp.left_shift(jnp.int32(1), jnp.int32(31) - i)
        n = count_ge(cand ^ sign)
        take = n >= k
        return jnp.where(take, cand, prefix), jnp.where(take, n, n_ge)

    prefix, n_ge = lax.fori_loop(0, 32, step, (jnp.zeros(shape, I32), jnp.full(shape, jnp.inf, F32)))
    return prefix ^ sign, n_ge


def _select_bias(s, idx, thr, cut):
    return jnp.where(s > thr, 0.0, jnp.where(s == thr, jnp.where(idx <= cut, 0.0, MASKED), MASKED))


def _stage_head_pairs(src_ref, dst_ref):
    lo = lax.broadcasted_iota(I32, (Q_TILE, LANES), 1) < HEAD_DIM
    for p in range(N_HEADS // 2):
        t = src_ref[:, p * LANES:(p + 1) * LANES]
        z = jnp.zeros_like(t)
        dst_ref[p, 0:Q_TILE, 0:LANES] = jnp.where(lo, t, z)
        dst_ref[p, Q_TILE:2 * Q_TILE, 0:LANES] = jnp.where(lo, z, t)


def _fold_rows(x, op):
    rows, lanes = x.shape
    return op(x.reshape(rows // ROW_FOLD, ROW_FOLD, lanes), axis=0)


def _bf16_exact(x):
    bits = lax.bitcast_convert_type(x, I32) & jnp.int32(-65536)
    return lax.bitcast_convert_type(bits, F32)


def _max_to_tile(x):
    x = _fold_rows(x, jnp.max)
    parts = [x[r:r + 8] for r in range(0, ROW_FOLD, 8)]
    while len(parts) > 1:
        parts = [jnp.maximum(parts[i], parts[i + 1]) for i in range(0, len(parts), 2)]
    return parts[0]


def _dsa_prompt_kernel(q_ref, qi_ref, wit_ref, k_ref, ki2_ref, vt_ref, o_ref,
                       s_ref, planes_ref, active_ref, qp_ref, qip_ref, acc_ref, peak_ref, thr_ref, aux_ref,
                       *, topk, idx_bits):
    j = pl.program_id(1)
    tk = KEY_TILE
    nch = ((j + 1) * Q_TILE + tk - 1) // tk
    npair = N_HEADS // 2

    _stage_head_pairs(qi_ref, qip_ref)
    _stage_head_pairs(q_ref, qp_ref)
    wt = wit_ref[...]
    wrow = [jnp.concatenate([wt[2 * p:2 * p + 1], wt[2 * p + 1:2 * p + 2]], axis=1) for p in range(npair)]
    qpos = j * Q_TILE + lax.broadcasted_iota(I32, (1, LANES), 1)
    krow = lax.broadcasted_iota(I32, (tk, LANES), 0)

    def tile(c):
        return pl.ds(pl.multiple_of(c * tk, tk), tk)

    def for_tiles(body):
        def two(i, carry):
            body(2 * i)
            body(2 * i + 1)
            return carry

        lax.fori_loop(0, nch // 2, two, 0)

        @pl.when(nch % 2 == 1)
        def _():
            body(nch - 1)

    def scores(c):
        kic = ki2_ref[tile(c), :]
        acc = None
        dots = [_dot_nt(kic, qip_ref[p]) for p in range(npair)]
        for p in range(npair):
            r = jnp.maximum(dots[p], 0.0) * wrow[p]
            r = r[:, :LANES] + r[:, LANES:]
            acc = r if acc is None else acc + r
        s_ref[tile(c), :] = jnp.where(krow <= qpos - c * tk, _ordered_key(acc), KEY_NEG_INF)

    for_tiles(scores)

    @pl.when(nch % 2 == 1)
    def _():
        s_ref[tile(nch), :] = jnp.full((tk, LANES), KEY_NEG_INF, I32)

    krow2 = lax.broadcasted_iota(I32, (2 * tk, LANES), 0)

    def tile2(c):
        return pl.ds(pl.multiple_of(c * (2 * tk), 2 * tk), 2 * tk)

    def count(pred):
        def body(c, acc):
            return acc + _fold_rows(pred(s_ref[tile2(c), :], c), jnp.sum)
        acc = lax.fori_loop(0, (nch + 1) // 2, body, jnp.zeros((ROW_FOLD, LANES), F32))
        return jnp.sum(acc, axis=0, keepdims=True)

    thr_ref[...] = jnp.full(thr_ref.shape, KEY_LOWEST, I32)

    @pl.when(j == 0)
    def _():
        planes_ref[...] = jnp.zeros(planes_ref.shape, I32)

    @pl.when((j + 1) * Q_TILE > topk)
    def _():
        row = (1, LANES)
        sign = jnp.int32(-2 ** 31)

        def build_planes(blk, carry):
            base = pl.multiple_of(blk * PLANE_BLOCK, PLANE_BLOCK)
            words = [s_ref[pl.ds(base + 8 * i, 8), :] ^ sign for i in range(32)]
            out = _bit_transpose32(words)
            group = pl.ds(pl.multiple_of(blk * 8, 8), 8)
            for r in range(32):
                planes_ref[r, group, :] = out[r]
            return carry

        lax.fori_loop(0, nch * (tk // PLANE_BLOCK), build_planes, 0)

        ngroups = active_ref.shape[0]
        nwords = nch * (tk // 32)

        def popcount_rows(words):
            return jnp.sum(lax.population_count(words).astype(F32), axis=0, keepdims=True)

        def descend(rows):
            grow = lax.broadcasted_iota(I32, (rows, LANES), 0)
            active_ref[0:rows, :] = jnp.where(grow < nwords, jnp.int32(-1), jnp.int32(0))

            def step(i, carry):
                key, need = carry
                active = active_ref[0:rows, :]
                ones = active & planes_ref[i, 0:rows, :]
                n_ones = popcount_rows(ones)
                take = n_ones >= need
                active_ref[0:rows, :] = jnp.where(take, ones, active ^ ones)
                bit = jnp.left_shift(jnp.int32(1), jnp.int32(31) - i)
                return jnp.where(take, key | bit, key), jnp.where(take, need, need - n_ones)

            key, need = lax.fori_loop(0, 32, step, (jnp.zeros(row, I32), jnp.full(row, float(topk), F32)))
            thr_ref[...] = jnp.broadcast_to(key ^ sign, thr_ref.shape)
            aux_ref[0:1, :] = need
            aux_ref[1:2, :] = popcount_rows(active_ref[0:rows, :])

        sizes = [ngroups // 4, ngroups // 2, ngroups] if ngroups % 32 == 0 else [ngroups]
        for n, rows in enumerate(sizes):
            fits = nwords <= rows
            if n > 0:
                fits = jnp.logical_and(fits, nwords > sizes[n - 1])
            if n == len(sizes) - 1 and n > 0:
                fits = nwords > sizes[n - 1]
            pl.when(fits)(functools.partial(descend, rows))

        thr = thr_ref[0:1, :]
        need = aux_ref[0:1, :]
        n_eq = aux_ref[1:2, :]

        @pl.when(jnp.max(n_eq - need) > 0.0)
        def _():
            cut = _tie_cut(
                lambda x: count(lambda s, c: jnp.where(s == thr, jnp.where(krow2 + c * (2 * tk) < x, 1.0, 0.0), 0.0)),
                need, row, idx_bits)

            def drop(c, carry):
                s = s_ref[tile2(c), :]
                s_ref[tile2(c), :] = jnp.where(s == thr, jnp.where(krow2 + c * (2 * tk) > cut, KEY_NEG_INF, s), s)
                return carry

            lax.fori_loop(0, (nch + 1) // 2, drop, 0)

    thr = thr_ref[0:1, :]

    lane = lax.broadcasted_iota(I32, (LANES, LANES), 1)
    ones = jnp.ones((ONES_ROWS, tk), BF16)

    def ones_columns(rows):
        return jnp.where(lax.broadcasted_iota(I32, (rows, LANES), 1) < N_HEADS, 1.0, 0.0).astype(BF16)

    def set_shifts(shifts):
        rows = [sh[:, half * LANES:(half + 1) * LANES] for sh in shifts for half in range(2)]
        cols = jnp.concatenate(rows + [jnp.zeros((LANES - len(rows), LANES), F32)], axis=0).T
        for i in range(len(rows)):
            qp_ref[i // 2, (i % 2) * Q_TILE:(i % 2 + 1) * Q_TILE, LANES:2 * LANES] = (
                jnp.where(lane == i, -cols, 0.0).astype(BF16))

    def shifted_scores(rows, p, bias):
        k_aug = jnp.concatenate([k_ref[rows, p * LANES:(p + 1) * LANES], ones_columns(bias.shape[0])], axis=1)
        return _dot_nt(k_aug, qp_ref[p]) + bias

    def attend_pass():
        acc_ref[...] = jnp.zeros(acc_ref.shape, F32)
        peak_ref[...] = jnp.full(peak_ref.shape, MASKED, F32)

        def attend(c):
            bias = jnp.where(s_ref[tile(c), :] >= thr, 0.0, MASKED)
            bias = jnp.concatenate([bias, bias], axis=1)
            ss = [shifted_scores(tile(c), p, bias) for p in range(npair)]
            for p in range(npair):
                peak_ref[p] = jnp.maximum(peak_ref[p], _max_to_tile(ss[p]))
                vt1 = jnp.concatenate([vt_ref[c, p * LANES:(p + 1) * LANES, :], ones], axis=0)
                acc_ref[p] += _dot(vt1, jnp.exp2(ss[p]).astype(BF16))

        for_tiles(attend)
        return [jnp.max(peak_ref[p], axis=0, keepdims=True) for p in range(npair)]

    bias0 = jnp.where(s_ref[0:LANES, :] > KEY_NEG_INF, 0.0, MASKED)
    bias0 = jnp.concatenate([bias0, bias0], axis=1)
    shifts = []
    for p in range(npair):
        qp_ref[p, :, LANES:2 * LANES] = jnp.zeros((2 * Q_TILE, LANES), BF16)
        sh = jnp.max(shifted_scores(pl.ds(0, LANES), p, bias0), axis=0, keepdims=True)
        shifts.append(_bf16_exact(sh))
    set_shifts(shifts)
    peaks = attend_pass()

    worst = jnp.abs(peaks[0])
    for p in range(1, npair):
        worst = jnp.maximum(worst, jnp.abs(peaks[p]))

    @pl.when(jnp.max(worst) > SHIFT_SLACK)
    def _():
        set_shifts([_bf16_exact(shifts[p] + peaks[p]) for p in range(npair)])
        attend_pass()

    even = lax.broadcasted_iota(I32, (LANES, LANES), 0) < HEAD_DIM
    for p in range(npair):
        o = acc_ref[p, 0:LANES, :] / acc_ref[p, LANES:LANES + 1, :]
        ot = jnp.where(even, o[:, :LANES], o[:, LANES:])
        o_ref[:, p * LANES:(p + 1) * LANES] = ot.T.astype(o_ref.dtype)


def _dsa_prompt(q, qi, wit, kb, ki2, vt, batch):
    n = q.shape[0]
    t = n // batch
    nq = t // Q_TILE
    ntk = t // KEY_TILE
    topk = min(TOPK_MAX, t // 4)
    assert topk % Q_TILE == 0
    idx_bits = max(1, math.ceil(math.log2(t)))
    once = dict(pipeline_mode=pl.Buffered(1))
    return pl.pallas_call(
        functools.partial(_dsa_prompt_kernel, topk=topk, idx_bits=idx_bits),
        out_shape=jax.ShapeDtypeStruct((n, D_ATTN), BF16),
        grid=(batch, nq),
        in_specs=[
            pl.BlockSpec((Q_TILE, D_ATTN), lambda b, j: (b * nq + j, 0)),
            pl.BlockSpec((Q_TILE, D_ATTN), lambda b, j: (b * nq + j, 0)),
            pl.BlockSpec((IDX_HEADS, Q_TILE), lambda b, j: (0, b * nq + j)),
            pl.BlockSpec((None, t, D_ATTN), lambda b, j: (b, 0, 0), **once),
            pl.BlockSpec((None, t, 2 * IDX_DIM), lambda b, j: (b, 0, 0), **once),
            pl.BlockSpec((None, ntk, D_ATTN, KEY_TILE), lambda b, j: (b, 0, 0, 0), **once),
        ],
        out_specs=pl.BlockSpec((Q_TILE, D_ATTN), lambda b, j: (b * nq + j, 0)),
        scratch_shapes=[
            pltpu.VMEM((t, LANES), I32),
            pltpu.VMEM((32, t // 32, LANES), I32),
            pltpu.VMEM((t // 32, LANES), I32),
            pltpu.VMEM((N_HEADS // 2, 2 * Q_TILE, 2 * LANES), BF16),
            pltpu.VMEM((N_HEADS // 2, 2 * Q_TILE, LANES), BF16),
            pltpu.VMEM((N_HEADS // 2, LANES + ONES_ROWS, 2 * LANES), F32),
            pltpu.VMEM((N_HEADS // 2, 8, 2 * LANES), F32),
            pltpu.VMEM((8, LANES), I32),
            pltpu.VMEM((8, LANES), F32),
        ],
        compiler_params=_params(("parallel", "arbitrary")),
        name="dsa_prompt",
    )(q, qi, wit, kb.reshape(batch, t, D_ATTN), ki2.reshape(batch, t, 2 * IDX_DIM),
      vt.reshape(batch, ntk, D_ATTN, KEY_TILE))


def _conv_tail(y, dwb_ref, lng_ref, lnb_ref, pww_ref):
    y = y + dwb_ref[...]
    mu = jnp.mean(y, axis=-1, keepdims=True)
    var = jnp.mean(jnp.square(y - mu), axis=-1, keepdims=True)
    y = (y - mu) * lax.rsqrt(var + EPS) * lng_ref[...] + lnb_ref[...]
    return _dot((y * _sigmoid(y)).astype(BF16), pww_ref[...])


def _merge(a, cv, yp, x, gate, wo_ref):
    m = _dot(a, wo_ref[0:D_ATTN, :])
    m = m + _dot(cv.astype(BF16), wo_ref[D_ATTN:D_ATTN + C_CONV, :])
    m = m + _dot(yp.astype(BF16), wo_ref[D_ATTN + C_CONV:, :])
    return x + gate * m


def _pool_window_lanes(shape):
    grp = lax.broadcasted_iota(I32, shape, len(shape) - 1) // POOL_GROUP
    w = jnp.full(shape, POOL_WINDOWS[-1], I32)
    for gi in range(len(POOL_WINDOWS) - 2, -1, -1):
        w = jnp.where(grp == gi, POOL_WINDOWS[gi], w)
    return grp, w


def _mix_kernel(a_ref, u_ref, uh_ref, p_ref, ph_ref, x_ref, gate_ref, dww_ref, dwb_ref, lng_ref, lnb_ref,
                pww_ref, pbd_ref, psc_ref, wo_ref, o_ref, ub_ref, pb_ref, pc_ref, *, tiles_per_seq):
    tm = u_ref.shape[0]
    i = pl.program_id(0)
    ti = i % tiles_per_seq
    first = ti == 0
    pad = 16

    ub_ref[0:HALO, :] = jnp.where(first, 0.0, uh_ref[...])
    ub_ref[HALO:HALO + tm, :] = u_ref[...]
    y = jnp.zeros((tm, C_CONV), F32)
    for tap in range(CONV_WIDTH):
        y = y + dww_ref[tap:tap + 1, :] * ub_ref[pl.ds(HALO - (CONV_WIDTH - 1) + tap, tm), :]
    cv = _conv_tail(y, dwb_ref, lng_ref, lnb_ref, pww_ref)

    n = HALO + tm
    pb_ref[0:pad, :] = jnp.zeros((pad, C_POOL), F32)
    pc_ref[0:pad, :] = jnp.zeros((pad, C_POOL), F32)
    pb_ref[pad:pad + HALO, :] = jnp.where(first, 0.0, ph_ref[...])
    pb_ref[pad + HALO:pad + n, :] = p_ref[...]
    cur = p_ref[...]
    grp, wl = _pool_window_lanes((tm, C_POOL))
    src, dst = pb_ref, pc_ref
    win = jnp.zeros((tm, C_POOL), F32)
    for gi, w in enumerate(POOL_WINDOWS):
        s = w // 2
        dst[pad:pad + n, :] = src[pl.ds(pad, n), :] + src[pl.ds(pad - s, n), :]
        win = jnp.where(grp == gi, dst[pl.ds(pad + HALO, tm), :], win)
        src, dst = dst, src
    pos = ti * tm + lax.broadcasted_iota(I32, (tm, C_POOL), 0)
    cnt = jnp.minimum(pos + 1, wl).astype(F32)
    d = win / cnt - cur
    yp = _dot(d.astype(BF16), pbd_ref[...]) * psc_ref[...]

    o_ref[...] = _merge(a_ref[...], cv, yp, x_ref[...], gate_ref[...], wo_ref)


def _mix(a, u, p, x, gate, dww, dwb, lng, lnb, pww, pbd, psc, wo, l, batch):
    n, d = x.shape
    tm = TOKEN_TILE
    tps = (n // batch) // tm
    hb = tm // HALO
    halo = lambda w: pl.BlockSpec((HALO, w), lambda i: (jnp.maximum(i * hb - 1, 0), 0))
    lay = lambda *s: pl.BlockSpec((None,) + s, lambda i: (l,) + (0,) * len(s))
    return pl.pallas_call(
        functools.partial(_mix_kernel, tiles_per_seq=tps),
        out_shape=jax.ShapeDtypeStruct((n, d), F32),
        grid=(n // tm,),
        in_specs=[
            pl.BlockSpec((tm, D_ATTN), lambda i: (i, 0)),
            pl.BlockSpec((tm, C_CONV), lambda i: (i, 0)), halo(C_CONV),
            pl.BlockSpec((tm, C_POOL), lambda i: (i, 0)), halo(C_POOL),
            pl.BlockSpec((tm, d), lambda i: (i, 0)),
            _mod_spec(gate, tps),
            lay(CONV_WIDTH, C_CONV), lay(1, C_CONV), lay(1, C_CONV), lay(1, C_CONV),
            lay(C_CONV, C_CONV), lay(C_POOL, C_POOL), lay(1, C_POOL), lay(d, d),
        ],
        out_specs=pl.BlockSpec((tm, d), lambda i: (i, 0)),
        scratch_shapes=[
            pltpu.VMEM((HALO + tm, C_CONV), F32),
            pltpu.VMEM((16 + HALO + tm, C_POOL), F32),
            pltpu.VMEM((16 + HALO + tm, C_POOL), F32),
        ],
        compiler_params=_params(("parallel",)),
        name="mix",
    )(a, u, u, p, p, x, gate, dww, dwb, lng, lnb, pww, pbd, psc, wo)


def _sample_scores_kernel(pt_ref, qi_ref, wi_ref, kin_ref, *rest):
    pages, o_ref = rest[:-1], rest[-1]
    qi = qi_ref[...]
    w = wi_ref[...]
    tiles = []
    for pg in pages:
        d = _dot(qi, pg[...].astype(BF16))
        tiles.append(jnp.sum(jnp.maximum(d, 0.0) * w, axis=0, keepdims=True))
    kin = kin_ref[...].astype(BF16).astype(F32)
    dn = jnp.sum(qi.astype(F32) * kin, axis=1, keepdims=True)
    sn = jnp.sum(jnp.maximum(dn, 0.0) * w, axis=0, keepdims=True)
    lane = lax.broadcasted_iota(I32, (1, LANES), 1)
    tiles.append(jnp.where(lane == 0, sn, -jnp.inf))
    o_ref[...] = jnp.concatenate(tiles, axis=1)


def _page_specs(n_pages, width, l):
    return [pl.BlockSpec((None, None, width, PAGE_SIZE), functools.partial(lambda n, pt, jj: (l, pt[n, jj], 0, 0), jj=jj))
            for jj in range(n_pages)]


def _sample_scores(page_table, qi, wit, kif, cache_ki, l):
    n, n_pages = page_table.shape
    width = n_pages * PAGE_SIZE + LANES
    per_seq = lambda *s: pl.BlockSpec((None,) + s, lambda i, pt: (i,) + (0,) * len(s))
    return pl.pallas_call(
        _sample_scores_kernel,
        out_shape=jax.ShapeDtypeStruct((n, 1, width), F32),
        grid_spec=pltpu.PrefetchScalarGridSpec(
            num_scalar_prefetch=1,
            grid=(n,),
            in_specs=[per_seq(IDX_HEADS, IDX_DIM), per_seq(IDX_HEADS, 1), per_seq(1, IDX_DIM)]
            + _page_specs(n_pages, IDX_DIM, l),
            out_specs=per_seq(1, width),
        ),
        compiler_params=_params(("arbitrary",)),
        name="sample_scores",
    )(page_table, qi.reshape(n, IDX_HEADS, IDX_DIM), wit.T.reshape(n, IDX_HEADS, 1),
      kif.reshape(n, 1, IDX_DIM), *([cache_ki] * n_pages))


def _sample_select_kernel(s_ref, o_ref, *, topk, idx_bits):
    s = s_ref[...]
    idx = lax.broadcasted_iota(I32, s.shape, 1)
    shape = (s.shape[0], 1)

    def count(x):
        return jnp.sum(x, axis=1, keepdims=True)

    thr, _ = _kth_largest(lambda t: count(jnp.where(s >= t, 1.0, 0.0)), shape, topk)
    eq = s == thr
    need = topk - count(jnp.where(s > thr, 1.0, 0.0))
    cut = _tie_cut(lambda x: count(jnp.where(eq, jnp.where(idx < x, 1.0, 0.0), 0.0)), need, shape, idx_bits)
    o_ref[...] = _select_bias(s, idx, thr, cut)


def _sample_select(scores, total):
    n, width = scores.shape
    topk = min(TOPK_MAX, total // 4)
    return pl.pallas_call(
        functools.partial(_sample_select_kernel, topk=topk, idx_bits=max(1, math.ceil(math.log2(width)))),
        out_shape=jax.ShapeDtypeStruct((n, width), F32),
        name="sample_select",
    )(scores)


def _sample_attn_kernel(pt_ref, q_ref, kn_ref, vn_ref, b_ref, *rest):
    n_pages = (len(rest) - 1) // 2
    kp, vp, o_ref = rest[:n_pages], rest[n_pages:2 * n_pages], rest[-1]
    past = n_pages * PAGE_SIZE
    own = (lax.broadcasted_iota(I32, (N_HEADS, D_ATTN), 0)
           == lax.broadcasted_iota(I32, (N_HEADS, D_ATTN), 1) // HEAD_DIM)
    qf = jnp.where(own, jnp.broadcast_to(q_ref[...].astype(F32), (N_HEADS, D_ATTN)), 0.0)
    qm = qf.astype(BF16)
    bias = b_ref[...]
    s = jnp.concatenate([_dot(qm, pg[...].astype(BF16)) for pg in kp], axis=1) + bias[:, :past]
    kn = kn_ref[...].astype(BF16).astype(F32)
    sn = jnp.sum(qf * kn, axis=1, keepdims=True) + bias[:, past:past + 1]
    m = jnp.maximum(jnp.max(s, axis=1, keepdims=True), sn)
    e = jnp.exp2(s - m)
    en = jnp.exp2(sn - m)
    den = jnp.sum(e, axis=1, keepdims=True) + en
    o = en * vn_ref[...].astype(BF16).astype(F32)
    eb = e.astype(BF16)
    for jj, pg in enumerate(vp):
        o = o + _dot_nt(eb[:, jj * PAGE_SIZE:(jj + 1) * PAGE_SIZE], pg[...].astype(BF16))
    o = o / den
    o_ref[...] = jnp.sum(jnp.where(own, o, 0.0), axis=0, keepdims=True).astype(o_ref.dtype)


def _sample_attn(page_table, q, kf, vf, bias, cache_k, cache_v, l):
    n, n_pages = page_table.shape
    width = bias.shape[-1]
    per_seq = lambda *s: pl.BlockSpec((None,) + s, lambda i, pt: (i,) + (0,) * len(s))
    return pl.pallas_call(
        _sample_attn_kernel,
        out_shape=jax.ShapeDtypeStruct((n, 1, D_ATTN), BF16),
        grid_spec=pltpu.PrefetchScalarGridSpec(
            num_scalar_prefetch=1,
            grid=(n,),
            in_specs=[per_seq(1, D_ATTN), per_seq(1, D_ATTN), per_seq(1, D_ATTN), per_seq(1, width)]
            + _page_specs(n_pages, D_ATTN, l) + _page_specs(n_pages, D_ATTN, l),
            out_specs=per_seq(1, D_ATTN),
        ),
        compiler_params=_params(("arbitrary",)),
        name="sample_attn",
    )(page_table, q.reshape(n, 1, D_ATTN), kf.reshape(n, 1, D_ATTN), vf.reshape(n, 1, D_ATTN),
      bias.reshape(n, 1, width), *([cache_k] * n_pages), *([cache_v] * n_pages))


def _sample_mix_kernel(a_ref, u_ref, sc_ref, p_ref, sp_ref, x_ref, gate_ref, dww_ref, dwb_ref, lng_ref, lnb_ref,
                       pww_ref, pbd_ref, psc_ref, wo_ref, o_ref, *, pos):
    u = u_ref[...]
    hist = CONV_WIDTH - 1
    y = dww_ref[hist:hist + 1, :] * u
    for tap in range(hist):
        y = y + dww_ref[tap:tap + 1, :] * sc_ref[tap]
    cv = _conv_tail(y, dwb_ref, lng_ref, lnb_ref, pww_ref)

    cur = p_ref[...]
    grp, wl = _pool_window_lanes(cur.shape)
    run = cur
    win = jnp.zeros_like(cur)
    back = 0
    for gi, w in enumerate(POOL_WINDOWS):
        while back < w - 1:
            back += 1
            run = run + sp_ref[POOL_HIST - back]
        win = jnp.where(grp == gi, run, win)
    cnt = jnp.minimum(pos + 1, wl).astype(F32)
    d = win / cnt - cur
    yp = _dot(d.astype(BF16), pbd_ref[...]) * psc_ref[...]

    o_ref[...] = _merge(a_ref[...], cv, yp, x_ref[...], gate_ref[...], wo_ref)


def _sample_mix(a, u, state_conv, p, state_pool, x, gate, dww, dwb, lng, lnb, pww, pbd, psc, wo, l, pos):
    n, d = x.shape
    full = lambda arr: pl.BlockSpec(arr.shape, lambda i: (0,) * arr.ndim)
    lay = lambda *s: pl.BlockSpec((None,) + s, lambda i: (l,) + (0,) * len(s))
    sc = jnp.swapaxes(state_conv, 1, 2)
    sp = jnp.swapaxes(state_pool, 1, 2)
    return pl.pallas_call(
        functools.partial(_sample_mix_kernel, pos=pos),
        out_shape=jax.ShapeDtypeStruct((n, d), F32),
        grid=(1,),
        in_specs=[
            full(a), full(u), lay(CONV_WIDTH - 1, n, C_CONV), full(p), lay(POOL_HIST, n, C_POOL), full(x),
            pl.BlockSpec((None, n, d), lambda i: (0, 0, 0)),
            lay(CONV_WIDTH, C_CONV), lay(1, C_CONV), lay(1, C_CONV), lay(1, C_CONV),
            lay(C_CONV, C_CONV), lay(C_POOL, C_POOL), lay(1, C_POOL), lay(d, d),
        ],
        out_specs=full(x),
        compiler_params=_params(("arbitrary",)),
        name="sample_mix",
    )(a, u, sc, p, sp, x, gate, dww, dwb, lng, lnb, pww, pbd, psc, wo)


def _final_norm_kernel(x_ref, g_ref, o_ref):
    x = x_ref[...]
    o_ref[...] = x * lax.rsqrt(jnp.mean(x * x, axis=-1, keepdims=True) + EPS) * g_ref[...]


def _final_norm(x, g):
    n, d = x.shape
    tm = min(TOKEN_TILE, n)
    return pl.pallas_call(
        _final_norm_kernel,
        out_shape=jax.ShapeDtypeStruct((n, d), F32),
        grid=(n // tm,),
        in_specs=[pl.BlockSpec((tm, d), lambda i: (i, 0)), pl.BlockSpec((1, d), lambda i: (0, 0))],
        out_specs=pl.BlockSpec((tm, d), lambda i: (i, 0)),
        compiler_params=_params(("parallel",)),
        name="final_norm",
    )(x, g)


def _prep_weights(w_in, w_out, conv_pw_w, pool_w, ffn_wg, ffn_wu, ffn_wd):
    ki = w_in[:, :, OFF_KI:OFF_WI]
    wcat = jnp.concatenate([
        w_in[:, :, OFF_Q:OFF_V], w_in[:, :, OFF_QI:OFF_KI], ki, ki,
        w_in[:, :, OFF_CONV:OFF_CONV + C_CONV], w_in[:, :, OFF_CONV + C_CONV:OFF_POOL], w_in[:, :, OFF_POOL:D_IN],
    ], axis=-1).astype(BF16)
    depth = pool_w.shape[0]
    wt = jnp.swapaxes(jnp.concatenate([
        w_in[:, :, OFF_K:OFF_QI], ki, w_in[:, :, OFF_WI:OFF_CONV],
        jnp.zeros((depth, w_in.shape[1], R_ALL - R_WI - IDX_HEADS), w_in.dtype),
    ], axis=-1), 1, 2).astype(BF16)
    pbd = jnp.zeros((depth, C_POOL, C_POOL), F32)
    for gi in range(len(POOL_WINDOWS)):
        sl = slice(gi * POOL_GROUP, (gi + 1) * POOL_GROUP)
        pbd = pbd.at[:, sl, sl].set(pool_w[:, gi])
    return (wcat, wt, w_out.astype(BF16), conv_pw_w.astype(BF16), pbd.astype(BF16),
            ffn_wg.astype(BF16), ffn_wu.astype(BF16), ffn_wd.astype(BF16))


def kernel(x_prompt, x_sample, cache_k, cache_v, cache_ki, state_conv, state_pool, page_table, c_prompt, c_sample, ada_w, ada_b, norm_g, ffn_wg, ffn_wu, ffn_wd, w_in, w_out, conv_dw_w, conv_dw_b, conv_ln_g, conv_ln_b, conv_pw_w, pool_w, pool_scale, final_g):
    batch, seq, d = x_prompt.shape
    nseq = x_sample.shape[0]
    depth = w_in.shape[0]
    n_pool = cache_k.shape[1]
    n_pages = page_table.shape[1]
    past = n_pages * PAGE_SIZE
    assert x_sample.shape[1] == 1 and d == D_MODEL
    assert seq % TOKEN_TILE == 0 and TOKEN_TILE == KEY_TILE and nseq % 8 == 0

    wcat, wt, wo, pww, pbd, wg, wu, wd = _prep_weights(w_in, w_out, conv_pw_w, pool_w, ffn_wg, ffn_wu, ffn_wd)
    ck = jnp.transpose(cache_k, (0, 1, 3, 4, 2)).reshape(depth, n_pool, D_ATTN, PAGE_SIZE)
    cv_ = jnp.transpose(cache_v, (0, 1, 3, 4, 2)).reshape(depth, n_pool, D_ATTN, PAGE_SIZE)
    cki = jnp.transpose(cache_ki, (0, 1, 3, 2))
    dwb = conv_dw_b.reshape(depth, 1, C_CONV)
    lng = conv_ln_g.reshape(depth, 1, C_CONV)
    lnb = conv_ln_b.reshape(depth, 1, C_CONV)
    psc = pool_scale.reshape(depth, 1, C_POOL)

    rows = batch + nseq
    pad = (-rows) % 8
    c_all = jnp.concatenate([c_prompt, c_sample, jnp.zeros((pad, d), F32)], axis=0)
    mods = _ada(c_all, ada_w, ada_b)

    xp = x_prompt.reshape(batch * seq, d)
    xs = x_sample.reshape(nseq, d)
    outs = [[] for _ in range(10)]
    for l in range(depth):
        mp = [mods[l, :batch, i * d:(i + 1) * d].reshape(batch, 1, d) for i in range(N_MOD)]
        ms = [mods[l, batch:rows, i * d:(i + 1) * d].reshape(1, nseq, d) for i in range(N_MOD)]
        g = [norm_g[l, i].reshape(1, d) for i in range(3)]
        mixw = (conv_dw_w, dwb, lng, lnb, pww, pbd, psc, wo, l)

        xp = _ffn(xp, g[0], mp[0], mp[1], mp[2], wg, wu, wd, l, 0)
        xs = _ffn(xs, g[0], ms[0], ms[1], ms[2], wg, wu, wd, l, 0)

        q, kb, ktf, vtf, vt, qi, kitf, ki2, wit, u, p = _inproj(xp, g[1], mp[3], mp[4], wcat, wt, l)
        a = _dsa_prompt(q, qi, wit, kb, ki2, vt, batch)
        xp = _mix(a, u, p, xp, mp[5], *mixw, batch)
        u3 = u.reshape(batch, seq, C_CONV)
        p3 = p.reshape(batch, seq, C_POOL)
        for lst, val in zip(outs[:5], (ktf, vtf, kitf, u3[:, seq - (CONV_WIDTH - 1):], p3[:, seq - POOL_HIST:])):
            lst.append(val)

        q, kb, ktf, vtf, vt, qi, kitf, ki2, wit, u, p = _inproj(xs, g[1], ms[3], ms[4], wcat, wt, l)
        scores = _sample_scores(page_table, qi, wit, kitf[0].T, cki, l)
        bias = _sample_select(scores.reshape(nseq, -1), past + 1)
        a = _sample_attn(page_table, q, ktf[0].T, vtf[0].T, bias, ck, cv_, l).reshape(nseq, D_ATTN)
        xs = _sample_mix(a, u, state_conv, p, state_pool, xs, ms[5], *mixw, past)
        for lst, val in zip(outs[5:], (ktf, vtf, kitf,
                                       jnp.concatenate([state_conv[l][:, 1:], u[:, None, :]], axis=1),
                                       jnp.concatenate([state_pool[l][:, 1:], p[:, None, :]], axis=1))):
            lst.append(val)

        xp = _ffn(xp, g[2], mp[6], mp[7], mp[8], wg, wu, wd, l, 1)
        xs = _ffn(xs, g[2], ms[6], ms[7], ms[8], wg, wu, wd, l, 1)

    fg = final_g.reshape(1, d)
    y_prompt = _final_norm(xp, fg).reshape(batch, seq, d)
    y_sample = _final_norm(xs, fg).reshape(nseq, 1, d)
    st = [jnp.stack(o) for o in outs]

    def heads_last(kt):
        return jnp.transpose(kt.reshape(depth, kt.shape[1], N_HEADS, HEAD_DIM, kt.shape[3]), (0, 1, 4, 2, 3))

    def per_seq(a):
        return jnp.swapaxes(a, 1, 2)

    return (y_prompt, y_sample,
            heads_last(st[0]), heads_last(st[1]), jnp.swapaxes(st[2], 2, 3), st[3], st[4],
            per_seq(heads_last(st[5])), per_seq(heads_last(st[6])), per_seq(jnp.swapaxes(st[7], 2, 3)), st[8], st[9])
```
